```python
import math
import jax, jax.numpy as jnp
from jax import lax
import numpy as np

D_MODEL = 1024
BATCH = 4
SEQ = 4096
DEPTH = 1
DEC_BATCH = 128
DEC_SEQ = 4
PAST_LEN = 8192
PAGE_SIZE = 128

HEAD_DIM = 64
N_HEADS_MOBA = (D_MODEL // 2) // HEAD_DIM
N_HEADS_DIFF = (D_MODEL // 2) // (2 * HEAD_DIM)
MOBA_WIDTH = N_HEADS_MOBA * HEAD_DIM
DIFF_WIDTH = N_HEADS_DIFF * 2 * HEAD_DIM
MIX_WIDTH = MOBA_WIDTH + DIFF_WIDTH
IN_WIDTH = 3 * MOBA_WIDTH + 3 * DIFF_WIDTH
MOBA_BLOCK = 256
MOBA_TOPK = 3
MOBA_Q_CHUNK = 32
DIFF_Q_CHUNK = 128
D_FF = ((8 * D_MODEL // 3 + 255) // 256) * 256
CONV_WIDTH = 3
PLE_DIM = 256
RMS_EPS = 1e-6
POOL_NUM = 5
POOL_DEN = 4

kernel_name = 'hybrid_moba_diffattn_convffn_step'


def rms_norm(x, g):
    xf = x.astype(jnp.float32)
    y = xf * lax.rsqrt(jnp.mean(xf * xf, axis=-1, keepdims=True) + RMS_EPS)
    return y.astype(x.dtype) * g


def alibi_slopes(n_heads):
    return jnp.asarray(2.0 ** (-8.0 * np.arange(1, n_heads + 1) / n_heads), dtype=jnp.float32)


def split_projection(z):
    B, S = z.shape[:2]
    cuts = [MOBA_WIDTH, 2 * MOBA_WIDTH, 3 * MOBA_WIDTH, 3 * MOBA_WIDTH + DIFF_WIDTH, 3 * MOBA_WIDTH + 2 * DIFF_WIDTH]
    qa, ka, va, qd, kd, vd = jnp.split(z, cuts, axis=-1)
    qa = qa.reshape(B, S, N_HEADS_MOBA, HEAD_DIM)
    ka = ka.reshape(B, S, N_HEADS_MOBA, HEAD_DIM)
    va = va.reshape(B, S, N_HEADS_MOBA, HEAD_DIM)
    qd = qd.reshape(B, S, N_HEADS_DIFF, 2, HEAD_DIM)
    kd = kd.reshape(B, S, N_HEADS_DIFF, 2, HEAD_DIM)
    vd = vd.reshape(B, S, N_HEADS_DIFF, 2 * HEAD_DIM)
    return qa, ka, va, qd, kd, vd


def map_query_blocks(fn, q, q_pos, block):
    B, S = q.shape[:2]
    n = S // block
    qb = jnp.moveaxis(q.reshape((B, n, block) + q.shape[2:]), 1, 0)
    pb = q_pos.reshape(n, block)
    out = lax.map(lambda a: fn(a[0], a[1]), (qb, pb))
    return jnp.moveaxis(out, 0, 1).reshape((B, S) + out.shape[3:])


def block_means(k):
    B, L = k.shape[:2]
    nb = -(-L // MOBA_BLOCK)
    kp = jnp.pad(k, ((0, 0), (0, nb * MOBA_BLOCK - L), (0, 0), (0, 0)))
    kp = kp.reshape((B, nb, MOBA_BLOCK) + k.shape[2:])
    return jnp.mean(kp, axis=2, dtype=jnp.float32).astype(k.dtype)


def gather_rows(t, pos):
    B, L, H = t.shape[:3]
    b = jnp.arange(B)[:, None, None, None]
    h = jnp.arange(H)[None, None, :, None]
    return t[b, jnp.clip(pos, 0, L - 1), h]


def gather_paged_rows(cache, layer, page_table, new, pos):
    B, n_pages = page_table.shape
    past_len = n_pages * PAGE_SIZE
    H = cache.shape[3]
    b = jnp.arange(B)[:, None, None, None]
    h = jnp.arange(H)[None, None, :, None]
    pc = jnp.clip(pos, 0, past_len - 1)
    past = cache[layer, page_table[b, pc // PAGE_SIZE], pc % PAGE_SIZE, h]
    fresh = new[b, jnp.clip(pos - past_len, 0, new.shape[1] - 1), h]
    return jnp.where((pos < past_len)[..., None], past, fresh)


def moba_attend(q, q_pos, k_means, fetch_k, fetch_v, slopes):
    B, Sq, H, dh = q.shape
    nb = k_means.shape[1]
    q_blk = q_pos // MOBA_BLOCK
    gate = jnp.einsum('bqhd,bnhd->bqhn', q, k_means).astype(jnp.float32)
    full_past = jnp.arange(nb)[None, :] < q_blk[:, None]
    gate = jnp.where(full_past[None, :, None, :], gate, -jnp.inf)
    top_val, top_idx = lax.top_k(gate, min(MOBA_TOPK, nb))
    own = jnp.broadcast_to(q_blk[None, :, None, None], (B, Sq, H, 1)).astype(top_idx.dtype)
    blk = jnp.concatenate([top_idx, own], axis=-1)
    blk_ok = jnp.concatenate([jnp.isfinite(top_val), jnp.ones(own.shape, bool)], axis=-1)
    pos = (blk[..., None] * MOBA_BLOCK + jnp.arange(MOBA_BLOCK)).reshape(B, Sq, H, -1)
    ok = jnp.repeat(blk_ok, MOBA_BLOCK, axis=-1) & (pos <= q_pos[None, :, None, None])
    k_rows = fetch_k(pos)
    v_rows = fetch_v(pos)
    s = jnp.einsum('bqhd,bqhrd->bqhr', q, k_rows).astype(jnp.float32) * (dh ** -0.5)
    s = s - slopes[None, None, :, None] * (q_pos[None, :, None, None] - pos).astype(jnp.float32)
    s = jnp.where(ok, s, -jnp.inf)
    p = jax.nn.softmax(s, axis=-1)
    return jnp.einsum('bqhr,bqhrd->bqhd', p.astype(v_rows.dtype), v_rows)


def diff_attend(q, k, v, q_pos, k_pos, lam, slopes):
    dh = q.shape[-1]
    s = jnp.einsum('bqhcd,bkhcd->bhcqk', q, k).astype(jnp.float32) * (dh ** -0.5)
    dist = (q_pos[:, None] - k_pos[None, :]).astype(jnp.float32)
    s = s - slopes[None, :, None, None, None] * dist
    s = jnp.where(dist >= 0, s, -jnp.inf)
    a = jax.nn.softmax(s, axis=-1)
    w = a[:, :, 0] - lam * a[:, :, 1]
    return jnp.einsum('bhqk,bkhe->bqhe', w.astype(v.dtype), v)


def conv_ffn(h, prev, w_gate, w_up, conv_w, conv_b, w_down):
    g = h @ w_gate
    u = h @ w_up
    g_ext = jnp.concatenate([prev.astype(g.dtype), g], axis=1)
    f = g.shape[-1]
    gc = lax.conv_general_dilated(g_ext, conv_w[:, None, :].astype(g.dtype), (1,), 'VALID',
                                  dimension_numbers=('NWC', 'WIO', 'NWC'), feature_group_count=f) + conv_b
    y = (jax.nn.gelu(gc) * u) @ w_down
    return y, g_ext[:, -(CONV_WIDTH - 1):]


def setup_inputs(seed: int = 0) -> dict:
    key = jax.random.key(seed)
    ks = jax.random.split(key, 32)
    f32 = jnp.float32
    n_pages = PAST_LEN // PAGE_SIZE
    n_pool = (DEC_BATCH * n_pages * POOL_NUM) // POOL_DEN

    def nrm(k, shape, scale=1.0):
        return jax.random.normal(k, shape, f32) * scale

    def gain(k, shape):
        return 1.0 + 0.01 * nrm(k, shape)

    perm = jax.random.permutation(ks[7], n_pool).astype(jnp.int32)
    page_table = perm[:DEC_BATCH * n_pages].reshape(DEC_BATCH, n_pages)
    return {
        'x_prompt': nrm(ks[0], (BATCH, SEQ, D_MODEL)),
        'x_sample': nrm(ks[1], (DEC_BATCH, DEC_SEQ, D_MODEL)),
        'cache_k_moba': nrm(ks[2], (DEPTH, n_pool, PAGE_SIZE, N_HEADS_MOBA, HEAD_DIM)),
        'cache_v_moba': nrm(ks[3], (DEPTH, n_pool, PAGE_SIZE, N_HEADS_MOBA, HEAD_DIM)),
        'cache_k_diff': nrm(ks[4], (DEPTH, n_pool, PAGE_SIZE, N_HEADS_DIFF, 2, HEAD_DIM)),
        'cache_v_diff': nrm(ks[5], (DEPTH, n_pool, PAGE_SIZE, N_HEADS_DIFF, 2 * HEAD_DIM)),
        'state_conv': nrm(ks[6], (DEPTH, DEC_BATCH, CONV_WIDTH - 1, D_FF)),
        'page_table': page_table,
        'p_prompt': nrm(ks[8], (DEPTH, BATCH, SEQ, PLE_DIM)),
        'p_sample': nrm(ks[9], (DEPTH, DEC_BATCH, DEC_SEQ, PLE_DIM)),
        'g_attn': gain(ks[10], (DEPTH, D_MODEL)),
        'w_in': nrm(ks[11], (DEPTH, D_MODEL, IN_WIDTH), D_MODEL ** -0.5),
        'lam_params': nrm(ks[12], (DEPTH, 4, HEAD_DIM), 0.1),
        'g_moba_out': gain(ks[13], (DEPTH, MOBA_WIDTH)),
        'g_diff_sub': gain(ks[14], (DEPTH, 2 * HEAD_DIM)),
        'w_o': nrm(ks[15], (DEPTH, MIX_WIDTH, D_MODEL), MIX_WIDTH ** -0.5),
        'g_ffn': gain(ks[16], (DEPTH, D_MODEL)),
        'w_gate': nrm(ks[17], (DEPTH, D_MODEL, D_FF), D_MODEL ** -0.5),
        'w_up': nrm(ks[18], (DEPTH, D_MODEL, D_FF), D_MODEL ** -0.5),
        'conv_w': nrm(ks[19], (DEPTH, CONV_WIDTH, D_FF), CONV_WIDTH ** -0.5),
        'conv_b': nrm(ks[20], (DEPTH, D_FF), 0.01),
        'w_down': nrm(ks[21], (DEPTH, D_FF, D_MODEL), D_FF ** -0.5),
        'g_ple': gain(ks[22], (DEPTH, D_MODEL)),
        'w_ple_gate': nrm(ks[23], (DEPTH, D_MODEL, D_MODEL), D_MODEL ** -0.5),
        'w_ple_proj': nrm(ks[24], (DEPTH, PLE_DIM, D_MODEL), PLE_DIM ** -0.5),
        'g_final': gain(ks[25], (D_MODEL,)),
    }


def reference(x_prompt, x_sample, cache_k_moba, cache_v_moba, cache_k_diff, cache_v_diff, state_conv,
              page_table, p_prompt, p_sample, g_attn, w_in, lam_params, g_moba_out, g_diff_sub, w_o,
              g_ffn, w_gate, w_up, conv_w, conv_b, w_down, g_ple, w_ple_gate, w_ple_proj, g_final):
    slopes_a = alibi_slopes(N_HEADS_MOBA)
    slopes_d = alibi_slopes(N_HEADS_DIFF)

    def attend_prompt(i, qa, ka, va, qd, kd, vd, lam):
        S = qa.shape[1]
        pos = jnp.arange(S, dtype=jnp.int32)
        means = block_means(ka)
        oa = map_query_blocks(
            lambda qc, pc: moba_attend(qc, pc, means, lambda r: gather_rows(ka, r),
                                       lambda r: gather_rows(va, r), slopes_a),
            qa, pos, MOBA_Q_CHUNK)
        od = map_query_blocks(
            lambda qc, pc: diff_attend(qc, kd, vd, pc, pos, lam, slopes_d),
            qd, pos, DIFF_Q_CHUNK)
        return oa, od

    def attend_sample(i, qa, ka, va, qd, kd, vd, lam):
        B, S = qa.shape[:2]
        past_len = page_table.shape[1] * PAGE_SIZE
        pos = past_len + jnp.arange(S, dtype=jnp.int32)
        ka_all = jnp.concatenate(
            [cache_k_moba[i, page_table].reshape(B, past_len, N_HEADS_MOBA, HEAD_DIM).astype(ka.dtype), ka], axis=1)
        means = block_means(ka_all)
        oa = map_query_blocks(
            lambda qc, pc: moba_attend(qc, pc, means, lambda r: gather_rows(ka_all, r),
                                       lambda r: gather_paged_rows(cache_v_moba, i, page_table, va, r).astype(va.dtype),
                                       slopes_a),
            qa, pos, 1)
        kd_all = jnp.concatenate(
            [cache_k_diff[i, page_table].reshape(B, past_len, N_HEADS_DIFF, 2, HEAD_DIM).astype(kd.dtype), kd], axis=1)
        vd_all = jnp.concatenate(
            [cache_v_diff[i, page_table].reshape(B, past_len, N_HEADS_DIFF, 2 * HEAD_DIM).astype(vd.dtype), vd], axis=1)
        k_pos = jnp.arange(past_len + S, dtype=jnp.int32)
        od = diff_attend(qd, kd_all, vd_all, pos, k_pos, lam, slopes_d)
        return oa, od

    def run_layer(i, x, p, conv_prev, attend):
        B, S = x.shape[:2]
        h = rms_norm(x, g_attn[i])
        qa, ka, va, qd, kd, vd = split_projection(h @ w_in[i])
        lp = lam_params[i].astype(jnp.float32)
        lam_init = 0.8 - 0.6 * math.exp(-0.3 * i)
        lam = jnp.exp(jnp.sum(lp[0] * lp[1])) - jnp.exp(jnp.sum(lp[2] * lp[3])) + lam_init
        oa, od = attend(i, qa, ka, va, qd, kd, vd, lam)
        oa = rms_norm(oa.reshape(B, S, MOBA_WIDTH), g_moba_out[i])
        od = (rms_norm(od, g_diff_sub[i]) * (1.0 - lam_init)).reshape(B, S, DIFF_WIDTH)
        x = x + jnp.concatenate([oa, od], axis=-1) @ w_o[i]
        f, conv_new = conv_ffn(rms_norm(x, g_ffn[i]), conv_prev, w_gate[i], w_up[i], conv_w[i], conv_b[i], w_down[i])
        x = x + f
        gate = jax.nn.sigmoid(rms_norm(x, g_ple[i]) @ w_ple_gate[i])
        x = x + (p[i] @ w_ple_proj[i]) * gate
        return x, ka, va, kd, vd, conv_new

    xp, xs = x_prompt, x_sample
    pk_a, pv_a, pk_d, pv_d, pc = [], [], [], [], []
    sk_a, sv_a, sk_d, sv_d, sc = [], [], [], [], []
    for i in range(DEPTH):
        zeros_prev = jnp.zeros((xp.shape[0], CONV_WIDTH - 1, D_FF), xp.dtype)
        xp, ka, va, kd, vd, cn = run_layer(i, xp, p_prompt, zeros_prev, attend_prompt)
        pk_a.append(ka); pv_a.append(va); pk_d.append(kd); pv_d.append(vd); pc.append(cn)
        xs, ka, va, kd, vd, cn = run_layer(i, xs, p_sample, state_conv[i], attend_sample)
        sk_a.append(ka); sv_a.append(va); sk_d.append(kd); sv_d.append(vd); sc.append(cn)

    y_prompt = rms_norm(xp, g_final)
    y_sample = rms_norm(xs, g_final)
    return (y_prompt, y_sample,
            jnp.stack(pk_a), jnp.stack(pv_a), jnp.stack(pk_d), jnp.stack(pv_d), jnp.stack(pc),
            jnp.stack(sk_a), jnp.stack(sv_a), jnp.stack(sk_d), jnp.stack(sv_d), jnp.stack(sc))
```

```python
import functools
import math

import jax
import jax.numpy as jnp
import numpy as np
from jax import lax
from jax.experimental import pallas as pl
from jax.experimental.pallas import tpu as pltpu

F32 = jnp.float32
BF16 = jnp.bfloat16

D_MODEL = 1024
HEAD_DIM = 64
N_HEADS_MOBA = 8
N_HEADS_DIFF = 4
N_KT_HEADS = 8
GROUP_WIDTH = 512
MOBA_BLOCK = 256
MOBA_TOPK = 3
PAGE_SIZE = 128
D_FF = 2816
FF_CHUNK = 256
N_FF_CHUNKS = D_FF // FF_CHUNK
PLE_DIM = 256
RMS_EPS = 1e-6
LAM_INIT = 0.8 - 0.6 * math.exp(-0.3 * 0)
QK_SCALE = HEAD_DIM ** -0.5
NEG_INF = float("-inf")

LANES = 128
SUBLANES = 8
VMEM_LIMIT = 56 * 1024 * 1024
PAGES_PER_STEP = 16
CHUNK_KEYS = PAGES_PER_STEP * PAGE_SIZE
HIGHEST = lax.Precision.HIGHEST


def _cparams(sem):
    return pltpu.CompilerParams(dimension_semantics=sem, vmem_limit_bytes=VMEM_LIMIT)


def _rms(x, g):
    return (x * lax.rsqrt(jnp.mean(x * x, axis=-1, keepdims=True) + RMS_EPS)) * g


def _dot_nt(a, b, precision=None):
    return lax.dot_general(a, b, (((1,), (1,)), ((), ())), precision=precision,
                           preferred_element_type=F32)


def _dot(a, b, precision=None):
    return jnp.dot(a, b, precision=precision, preferred_element_type=F32)


def _alibi_slopes(n_heads):
    return jnp.asarray(2.0 ** (-8.0 * np.arange(1, n_heads + 1) / n_heads), dtype=F32)


def _top3_select(gate, valid, n_blocks):
    blk = lax.broadcasted_iota(jnp.int32, gate.shape, 1)
    g = jnp.where(valid, gate, NEG_INF)
    sel = jnp.zeros(gate.shape, jnp.bool_)
    for _ in range(MOBA_TOPK):
        m = jnp.max(g, axis=1, keepdims=True)
        first = jnp.min(jnp.where(g == m, blk, n_blocks), axis=1, keepdims=True)
        pick = (blk == first) & (m > NEG_INF)
        sel = sel | (pick & (m < float("inf")))
        g = jnp.where(pick, NEG_INF, g)
    return sel


def _flash_update(state, s, pv):
    m_i, l_i, acc = state
    m_new = jnp.maximum(m_i, jnp.max(s, axis=1, keepdims=True))
    alpha = jnp.exp(m_i - m_new)
    p = jnp.exp(s - m_new)
    l_new = alpha * l_i + jnp.sum(p, axis=1, keepdims=True)
    return m_new, l_new, alpha * acc + pv(p.astype(BF16))


def _flash_init(rows, width):
    return (jnp.full((rows, 1), NEG_INF, F32), jnp.zeros((rows, 1), F32), jnp.zeros((rows, width), F32))


def _proj_kernel(x_ref, g_ref, wr_ref, *rest, n_rm, n_t):
    if n_t:
        wt_ref, rest = rest[0], rest[1:]
    h = _rms(x_ref[0], g_ref[...]).astype(BF16)
    tm = h.shape[0]
    for idx in range(n_rm):
        rest[idx][0] = _dot(h, wr_ref[:, idx * GROUP_WIDTH:(idx + 1) * GROUP_WIDTH])
    for idx in range(n_t):
        zt = _dot_nt(wt_ref[idx * GROUP_WIDTH:(idx + 1) * GROUP_WIDTH, :], h)
        rest[n_rm + idx][0] = zt.reshape(N_KT_HEADS, HEAD_DIM, tm)


def _proj(x3d, g, wr, wt, tm):
    nb, s, _ = x3d.shape
    n_rm = wr.shape[1] // GROUP_WIDTH
    n_t = 0 if wt is None else wt.shape[0] // GROUP_WIDTH
    const = lambda shape: pl.BlockSpec(shape, lambda bi, i: (0,) * len(shape))
    in_specs = [pl.BlockSpec((1, tm, D_MODEL), lambda bi, i: (bi, i, 0)), const((1, D_MODEL)), const(wr.shape)]
    args = [x3d, g, wr]
    if n_t:
        in_specs.append(const(wt.shape))
        args.append(wt)
    return pl.pallas_call(
        functools.partial(_proj_kernel, n_rm=n_rm, n_t=n_t),
        grid=(nb, s // tm),
        in_specs=in_specs,
        out_specs=[pl.BlockSpec((1, tm, GROUP_WIDTH), lambda bi, i: (bi, i, 0))] * n_rm
        + [pl.BlockSpec((1, N_KT_HEADS, HEAD_DIM, tm), lambda bi, i: (bi, 0, 0, i))] * n_t,
        out_shape=[jax.ShapeDtypeStruct((nb, s, GROUP_WIDTH), F32)] * n_rm
        + [jax.ShapeDtypeStruct((nb, N_KT_HEADS, HEAD_DIM, s), F32)] * n_t,
        compiler_params=_cparams(("arbitrary", "arbitrary")),
        name="proj",
    )(*args)


def _moba_prompt_kernel(slopes_ref, q_ref, kt_ref, vt_ref, o_ref, kb_ref, vb_ref, mean_ref, *, n_blocks):
    hp = pl.program_id(1)
    i = pl.program_id(2)
    tq = MOBA_BLOCK

    @pl.when(i == 0)
    def _():
        blk_lane = lax.broadcasted_iota(jnp.int32, (HEAD_DIM, n_blocks), 1)
        for hh in range(2):
            means = jnp.zeros((HEAD_DIM, n_blocks), F32)
            for j in range(n_blocks):
                kj = kt_ref[0, hh, :, j * MOBA_BLOCK:(j + 1) * MOBA_BLOCK]
                kb_ref[hh * n_blocks + j] = kj.astype(BF16)
                vb_ref[hh * n_blocks + j] = vt_ref[0, hh, :, j * MOBA_BLOCK:(j + 1) * MOBA_BLOCK].astype(BF16)
                means = jnp.where(blk_lane == j, jnp.sum(kj, axis=1, keepdims=True) / MOBA_BLOCK, means)
            mean_ref[hh] = means

    q = q_ref[0]
    row = lax.broadcasted_iota(jnp.int32, (tq, MOBA_BLOCK), 0)
    col = lax.broadcasted_iota(jnp.int32, (tq, MOBA_BLOCK), 1)
    rel = (row - col).astype(F32)
    blk = lax.broadcasted_iota(jnp.int32, (tq, n_blocks), 1)
    outs = []
    for hh in range(2):
        qh = q[:, hh * HEAD_DIM:(hh + 1) * HEAD_DIM]
        slope = slopes_ref[hp * 2 + hh]
        gate = _dot(qh, mean_ref[hh], precision=HIGHEST)
        sel = _top3_select(gate, blk < i, n_blocks)
        sel_bias = jnp.where(sel, 0.0, NEG_INF)
        qb = (qh * QK_SCALE).astype(BF16)
        base = hh * n_blocks

        def body(j, state, qb=qb, slope=slope, sel_bias=sel_bias, base=base):
            s = _dot(qb, kb_ref[base + j])
            s = s - slope * (rel + ((i - j) * MOBA_BLOCK).astype(F32))
            s = s + jnp.max(jnp.where(blk == j, sel_bias, NEG_INF), axis=1, keepdims=True)
            return _flash_update(state, s, lambda p: _dot_nt(p, vb_ref[base + j]))

        s = _dot(qb, kb_ref[base + i]) - slope * rel
        s = jnp.where(row >= col, s, NEG_INF)
        state = _flash_update(_flash_init(tq, HEAD_DIM), s, lambda p: _dot_nt(p, vb_ref[base + i]))
        _, l_i, acc = lax.fori_loop(0, i, body, state)
        outs.append(acc / l_i)
    o_ref[0] = jnp.concatenate(outs, axis=1)


def _moba_prompt(slopes, qa, kat, vat):
    b, s, _ = qa.shape
    n_blocks = s // MOBA_BLOCK
    blk_spec = pl.BlockSpec((1, MOBA_BLOCK, LANES), lambda bi, hp, i, sl: (bi, i, hp))
    kt_spec = pl.BlockSpec((1, 2, HEAD_DIM, s), lambda bi, hp, i, sl: (bi, hp, 0, 0))
    return pl.pallas_call(
        functools.partial(_moba_prompt_kernel, n_blocks=n_blocks),
        grid_spec=pltpu.PrefetchScalarGridSpec(
            num_scalar_prefetch=1,
            grid=(b, N_HEADS_MOBA // 2, n_blocks),
            in_specs=[blk_spec, kt_spec, kt_spec],
            out_specs=blk_spec,
            scratch_shapes=[pltpu.VMEM((2 * n_blocks, HEAD_DIM, MOBA_BLOCK), BF16),
                            pltpu.VMEM((2 * n_blocks, HEAD_DIM, MOBA_BLOCK), BF16),
                            pltpu.VMEM((2, HEAD_DIM, n_blocks), F32)]),
        out_shape=jax.ShapeDtypeStruct((b, s, GROUP_WIDTH), F32),
        compiler_params=_cparams(("arbitrary", "arbitrary", "arbitrary")),
        name="moba_prompt",
    )(slopes, qa, kat, vat)


def _lambda(lp_ref):
    lp = lp_ref[...]
    a = jnp.sum(lp[0:1] * lp[1:2], axis=1, keepdims=True)
    c = jnp.sum(lp[2:3] * lp[3:4], axis=1, keepdims=True)
    return jnp.exp(a) - jnp.exp(c) + LAM_INIT


def _diff_prompt_kernel(slopes_ref, lp_ref, q_ref, kt_ref, v_ref, o_ref, kb_ref, vb_ref, *, n_blocks):
    h = pl.program_id(1)
    i = pl.program_id(2)
    tq = MOBA_BLOCK

    @pl.when(i == 0)
    def _():
        vb_ref[...] = v_ref[0].astype(BF16)
        for c in range(2):
            for j in range(n_blocks):
                kb_ref[c * n_blocks + j] = kt_ref[0, c, :, j * tq:(j + 1) * tq].astype(BF16)

    q = q_ref[0] * QK_SCALE
    row = lax.broadcasted_iota(jnp.int32, (tq, tq), 0)
    col = lax.broadcasted_iota(jnp.int32, (tq, tq), 1)
    rel = (row - col).astype(F32)
    slope = slopes_ref[h]
    qs = [q[:, c * HEAD_DIM:(c + 1) * HEAD_DIM].astype(BF16) for c in range(2)]

    def body(j, states):
        vj = vb_ref[pl.ds(pl.multiple_of(j * tq, tq), tq), :]
        bias = slope * (rel + ((i - j) * tq).astype(F32))
        return tuple(_flash_update(states[c], _dot(qs[c], kb_ref[c * n_blocks + j]) - bias, lambda p: _dot(p, vj))
                     for c in range(2))

    vi = vb_ref[pl.ds(pl.multiple_of(i * tq, tq), tq), :]
    causal = row >= col
    bias = slope * rel
    states = tuple(
        _flash_update(_flash_init(tq, 2 * HEAD_DIM),
                      jnp.where(causal, _dot(qs[c], kb_ref[c * n_blocks + i]) - bias, NEG_INF), lambda p: _dot(p, vi))
        for c in range(2))
    st1, st2 = lax.fori_loop(0, i, body, states)
    lam = _lambda(lp_ref)
    o_ref[0] = st1[2] / st1[1] - lam * (st2[2] / st2[1])


def _diff_prompt(slopes, lam_params, qd, kdt, vd):
    b, s, _ = qd.shape
    n_blocks = s // MOBA_BLOCK
    blk_spec = pl.BlockSpec((1, MOBA_BLOCK, LANES), lambda bi, h, i, sl: (bi, i, h))
    return pl.pallas_call(
        functools.partial(_diff_prompt_kernel, n_blocks=n_blocks),
        grid_spec=pltpu.PrefetchScalarGridSpec(
            num_scalar_prefetch=1,
            grid=(b, N_HEADS_DIFF, n_blocks),
            in_specs=[pl.BlockSpec((4, HEAD_DIM), lambda bi, h, i, sl: (0, 0)), blk_spec,
                      pl.BlockSpec((1, 2, HEAD_DIM, s), lambda bi, h, i, sl: (bi, h, 0, 0)),
                      pl.BlockSpec((1, s, LANES), lambda bi, h, i, sl: (bi, 0, h))],
            out_specs=blk_spec,
            scratch_shapes=[pltpu.VMEM((2 * n_blocks, HEAD_DIM, MOBA_BLOCK), BF16), pltpu.VMEM((s, LANES), BF16)]),
        out_shape=jax.ShapeDtypeStruct((b, s, GROUP_WIDTH), F32),
        compiler_params=_cparams(("arbitrary", "arbitrary", "arbitrary")),
        name="diff_prompt",
    )(slopes, lam_params, qd, kdt, vd)


def _block_diag_q(q4):
    n_q = q4.shape[0]
    rows = N_KT_HEADS * n_q
    tiled = jnp.concatenate([q4] * N_KT_HEADS, axis=0)
    r = lax.broadcasted_iota(jnp.int32, (rows, GROUP_WIDTH), 0)
    c = lax.broadcasted_iota(jnp.int32, (rows, GROUP_WIDTH), 1)
    return jnp.where(r // n_q == c // HEAD_DIM, tiled, 0.0)


def _pad_rows(x, rows):
    return jnp.concatenate([x, jnp.zeros((rows - x.shape[0], x.shape[1]), x.dtype)], axis=0)


def _row_slopes(slopes_ref, head_of_row, n_heads):
    slope = jnp.zeros(head_of_row.shape, F32)
    for hh in range(n_heads):
        slope = jnp.where(head_of_row == hh, slopes_ref[hh], slope)
    return slope


def _new_key_scores(qb, kn_ref, slope, t, n_q):
    kn = _pad_rows(kn_ref[0], SUBLANES)
    sn = _dot_nt(qb, kn.astype(BF16))
    u = lax.broadcasted_iota(jnp.int32, sn.shape, 1)
    return jnp.where((u <= t) & (u < n_q), sn - slope * (t - u).astype(F32), NEG_INF)


def _moba_sample_kernel(pt_ref, slopes_ref, q_ref, kn_ref, vn_ref, *rest, n_q, n_chunks):
    k_pages = rest[:PAGES_PER_STEP]
    v_pages = rest[PAGES_PER_STEP:2 * PAGES_PER_STEP]
    o_ref = rest[2 * PAGES_PER_STEP]
    s_ref, p_ref, gate_ref, acc_ref, l_ref, pn_ref = rest[2 * PAGES_PER_STEP + 1:]
    c = pl.program_id(1)
    rows = N_HEADS_MOBA * n_q
    n_blocks = n_chunks * CHUNK_KEYS // MOBA_BLOCK
    blocks_per_step = CHUNK_KEYS // MOBA_BLOCK
    past_len = n_chunks * CHUNK_KEYS
    qb = (_block_diag_q(q_ref[0]) * QK_SCALE).astype(BF16)

    @pl.when(c < n_chunks)
    def _():
        blk_lane = lax.broadcasted_iota(jnp.int32, (rows, n_blocks), 1)
        gate = jnp.where(c == 0, 0.0, gate_ref[...])
        for p in range(PAGES_PER_STEP):
            kp = k_pages[p][0].reshape(GROUP_WIDTH, PAGE_SIZE)
            sp = _dot(qb, kp.astype(BF16))
            s_ref[:, pl.ds(pl.multiple_of(c * CHUNK_KEYS + p * PAGE_SIZE, PAGE_SIZE), PAGE_SIZE)] = sp
            if p % 2 == 0:
                pair = sp
            else:
                bsum = jnp.sum(pair + sp, axis=1, keepdims=True) / MOBA_BLOCK
                gate = jnp.where(blk_lane == c * blocks_per_step + p // 2, bsum, gate)
        gate_ref[...] = gate

    @pl.when(c == n_chunks - 1)
    def _():
        gate = gate_ref[...]
        sel = _top3_select(gate, jnp.ones(gate.shape, jnp.bool_), n_blocks)
        eb = lax.broadcasted_iota(jnp.int32, (n_blocks, past_len), 0)
        ek = lax.broadcasted_iota(jnp.int32, (n_blocks, past_len), 1)
        expand = jnp.where(ek // MOBA_BLOCK == eb, 1.0, 0.0).astype(BF16)
        selk = _dot(jnp.where(sel, 1.0, 0.0).astype(BF16), expand)
        r = lax.broadcasted_iota(jnp.int32, (rows, 1), 0)
        t = r % n_q
        slope = _row_slopes(slopes_ref, r // n_q, N_HEADS_MOBA)
        kpos = lax.broadcasted_iota(jnp.int32, (rows, past_len), 1)
        s = jnp.where(selk > 0.5, s_ref[...] - slope * (past_len + t - kpos).astype(F32), NEG_INF)
        sn = _new_key_scores(qb, kn_ref, slope, t, n_q)
        m = jnp.maximum(jnp.max(s, axis=1, keepdims=True), jnp.max(sn, axis=1, keepdims=True))
        p = jnp.exp(s - m)
        pn = jnp.exp(sn - m)
        l_ref[...] = jnp.sum(p, axis=1, keepdims=True) + jnp.sum(pn, axis=1, keepdims=True)
        p_ref[...] = p.astype(BF16)
        pn_ref[...] = pn
        acc_ref[...] = jnp.zeros(acc_ref.shape, F32)

    @pl.when(c >= n_chunks)
    def _():
        cv = c - n_chunks
        acc = acc_ref[...]
        for p in range(PAGES_PER_STEP):
            pp = p_ref[:, pl.ds(pl.multiple_of(cv * CHUNK_KEYS + p * PAGE_SIZE, PAGE_SIZE), PAGE_SIZE)]
            vp = v_pages[p][0].reshape(GROUP_WIDTH, PAGE_SIZE)
            acc = acc + _dot_nt(pp, vp.astype(BF16))
        acc_ref[...] = acc

    @pl.when(c == 2 * n_chunks - 1)
    def _():
        vn = _pad_rows(vn_ref[0], SUBLANES)
        acc = acc_ref[...] + _dot(pn_ref[...].astype(BF16), vn.astype(BF16))
        acc = acc / l_ref[...]
        r = lax.broadcasted_iota(jnp.int32, (rows, GROUP_WIDTH), 0)
        cc = lax.broadcasted_iota(jnp.int32, (rows, GROUP_WIDTH), 1)
        acc = jnp.where(r // n_q == cc // HEAD_DIM, acc, 0.0)
        out = acc[0:n_q]
        for hh in range(1, N_HEADS_MOBA):
            out = out + acc[hh * n_q:(hh + 1) * n_q]
        o_ref[0] = out


def _page_specs(block_shape, n_chunks, phase_offset):
    zeros = (0,) * (len(block_shape) - 1)

    def make(p):
        def index_map(bi, c, pt, sl):
            chunk = jnp.clip(c - phase_offset, 0, n_chunks - 1)
            return (pt[bi, chunk * PAGES_PER_STEP + p],) + zeros
        return pl.BlockSpec(block_shape, index_map)
    return [make(p) for p in range(PAGES_PER_STEP)]


KT_PAGE = (1, N_KT_HEADS, HEAD_DIM, PAGE_SIZE)


def _moba_sample(page_table, slopes, qa, ka_new, va_new, cache_kt, cache_vt):
    b, n_q, _ = qa.shape
    n_pages = page_table.shape[1]
    n_chunks = n_pages // PAGES_PER_STEP
    past_len = n_pages * PAGE_SIZE
    rows = N_HEADS_MOBA * n_q
    tok_spec = pl.BlockSpec((1, n_q, GROUP_WIDTH), lambda bi, c, pt, sl: (bi, 0, 0))
    return pl.pallas_call(
        functools.partial(_moba_sample_kernel, n_q=n_q, n_chunks=n_chunks),
        grid_spec=pltpu.PrefetchScalarGridSpec(
            num_scalar_prefetch=2,
            grid=(b, 2 * n_chunks),
            in_specs=[tok_spec, tok_spec, tok_spec] + _page_specs(KT_PAGE, n_chunks, 0)
            + _page_specs(KT_PAGE, n_chunks, n_chunks),
            out_specs=tok_spec,
            scratch_shapes=[pltpu.VMEM((rows, past_len), F32), pltpu.VMEM((rows, past_len), BF16),
                            pltpu.VMEM((rows, past_len // MOBA_BLOCK), F32),
                            pltpu.VMEM((rows, GROUP_WIDTH), F32), pltpu.VMEM((rows, 1), F32),
                            pltpu.VMEM((rows, SUBLANES), F32)]),
        out_shape=jax.ShapeDtypeStruct((b, n_q, GROUP_WIDTH), F32),
        compiler_params=_cparams(("arbitrary", "arbitrary")),
        name="moba_sample",
    )(page_table, slopes, qa, ka_new, va_new, *([cache_kt] * PAGES_PER_STEP), *([cache_vt] * PAGES_PER_STEP))


def _diff_sample_kernel(pt_ref, slopes_ref, lp_ref, q_ref, kn_ref, vn_ref, *rest, n_q, n_chunks):
    k_pages = rest[:PAGES_PER_STEP]
    v_pages = rest[PAGES_PER_STEP:2 * PAGES_PER_STEP]
    o_ref = rest[2 * PAGES_PER_STEP]
    s_ref, m_ref, l_ref, acc_ref = rest[2 * PAGES_PER_STEP + 1:]
    c = pl.program_id(1)
    rows = N_KT_HEADS * n_q
    head_rows = 2 * n_q
    e = 2 * HEAD_DIM
    past_len = n_chunks * CHUNK_KEYS
    qb = (_block_diag_q(q_ref[0]) * QK_SCALE).astype(BF16)

    @pl.when(c == 0)
    def _():
        m_ref[...] = jnp.full(m_ref.shape, NEG_INF, F32)
        l_ref[...] = jnp.zeros(l_ref.shape, F32)
        acc_ref[...] = jnp.zeros(acc_ref.shape, F32)

    r = lax.broadcasted_iota(jnp.int32, (rows, 1), 0)
    t = r % n_q
    slope = _row_slopes(slopes_ref, r // head_rows, N_HEADS_DIFF)

    for p in range(PAGES_PER_STEP):
        kp = k_pages[p][0].reshape(GROUP_WIDTH, PAGE_SIZE)
        s_ref[:, p * PAGE_SIZE:(p + 1) * PAGE_SIZE] = _dot(qb, kp.astype(BF16))
    kpos = c * CHUNK_KEYS + lax.broadcasted_iota(jnp.int32, (rows, CHUNK_KEYS), 1)
    s = s_ref[...] - slope * (past_len + t - kpos).astype(F32)
    m_old = m_ref[...]
    m_new = jnp.maximum(m_old, jnp.max(s, axis=1, keepdims=True))
    alpha = jnp.exp(m_old - m_new)
    pexp = jnp.exp(s - m_new)
    l_ref[...] = alpha * l_ref[...] + jnp.sum(pexp, axis=1, keepdims=True)
    m_ref[...] = m_new
    pb = pexp.astype(BF16)
    acc = alpha * acc_ref[...]
    parts = []
    for hh in range(N_HEADS_DIFF):
        a = acc[hh * head_rows:(hh + 1) * head_rows]
        for p in range(PAGES_PER_STEP):
            vh = v_pages[p][0, :, hh, :].astype(BF16)
            a = a + _dot(pb[hh * head_rows:(hh + 1) * head_rows, p * PAGE_SIZE:(p + 1) * PAGE_SIZE], vh)
        parts.append(a)
    acc_ref[...] = jnp.concatenate(parts, axis=0)

    @pl.when(c == n_chunks - 1)
    def _():
        sn = _new_key_scores(qb, kn_ref, slope, t, n_q)
        m_o = m_ref[...]
        m_f = jnp.maximum(m_o, jnp.max(sn, axis=1, keepdims=True))
        al = jnp.exp(m_o - m_f)
        pn = jnp.exp(sn - m_f)
        l_f = al * l_ref[...] + jnp.sum(pn, axis=1, keepdims=True)
        pnb = pn.astype(BF16)
        vn = _pad_rows(vn_ref[0], SUBLANES).astype(BF16)
        a_old = al * acc_ref[...]
        lam = _lambda(lp_ref)
        outs = []
        for hh in range(N_HEADS_DIFF):
            rs = slice(hh * head_rows, (hh + 1) * head_rows)
            a = (a_old[rs] + _dot(pnb[rs], vn[:, hh * e:(hh + 1) * e])) / l_f[rs]
            outs.append(a[0:n_q] - lam * a[n_q:2 * n_q])
        o_ref[0] = jnp.concatenate(outs, axis=1)


def _diff_sample(page_table, slopes, lam_params, qd, kd_new, vd_new, cache_kt, cache_v):
    b, n_q, _ = qd.shape
    n_pages = page_table.shape[1]
    n_chunks = n_pages // PAGES_PER_STEP
    rows = N_KT_HEADS * n_q
    tok_spec = pl.BlockSpec((1, n_q, GROUP_WIDTH), lambda bi, c, pt, sl: (bi, 0, 0))
    v_page = (1, PAGE_SIZE, N_HEADS_DIFF, 2 * HEAD_DIM)
    return pl.pallas_call(
        functools.partial(_diff_sample_kernel, n_q=n_q, n_chunks=n_chunks),
        grid_spec=pltpu.PrefetchScalarGridSpec(
            num_scalar_prefetch=2,
            grid=(b, n_chunks),
            in_specs=[pl.BlockSpec((4, HEAD_DIM), lambda bi, c, pt, sl: (0, 0)), tok_spec, tok_spec, tok_spec]
            + _page_specs(KT_PAGE, n_chunks, 0) + _page_specs(v_page, n_chunks, 0),
            out_specs=tok_spec,
            scratch_shapes=[pltpu.VMEM((rows, CHUNK_KEYS), F32), pltpu.VMEM((rows, 1), F32),
                            pltpu.VMEM((rows, 1), F32), pltpu.VMEM((rows, 2 * HEAD_DIM), F32)]),
        out_shape=jax.ShapeDtypeStruct((b, n_q, GROUP_WIDTH), F32),
        compiler_params=_cparams(("arbitrary", "arbitrary")),
        name="diff_sample",
    )(page_table, slopes, lam_params, qd, kd_new, vd_new,
      *([cache_kt] * PAGES_PER_STEP), *([cache_v] * PAGES_PER_STEP))


def _post_kernel(oa_ref, od_ref, x_ref, gm_ref, gd_ref, wo_ref, o_ref):
    oa = _rms(oa_ref[...], gm_ref[...]).astype(BF16)
    y = x_ref[...] + _dot(oa, wo_ref[0:GROUP_WIDTH, :])
    od = od_ref[...]
    for h in range(N_HEADS_DIFF):
        w = 2 * HEAD_DIM
        odh = (_rms(od[:, h * w:(h + 1) * w], gd_ref[...]) * (1.0 - LAM_INIT)).astype(BF16)
        y = y + _dot(odh, wo_ref[GROUP_WIDTH + h * w:GROUP_WIDTH + (h + 1) * w, :])
    o_ref[...] = y


def _post(oa, od, x2d, g_moba, g_diff, wo_bf16, tm):
    t = x2d.shape[0]
    return pl.pallas_call(
        _post_kernel,
        grid=(t // tm,),
        in_specs=[pl.BlockSpec((tm, GROUP_WIDTH), lambda i: (i, 0)),
                  pl.BlockSpec((tm, GROUP_WIDTH), lambda i: (i, 0)),
                  pl.BlockSpec((tm, D_MODEL), lambda i: (i, 0)),
                  pl.BlockSpec((1, GROUP_WIDTH), lambda i: (0, 0)),
                  pl.BlockSpec((1, 2 * HEAD_DIM), lambda i: (0, 0)),
                  pl.BlockSpec((D_MODEL, D_MODEL), lambda i: (0, 0))],
        out_specs=pl.BlockSpec((tm, D_MODEL), lambda i: (i, 0)),
        out_shape=jax.ShapeDtypeStruct((t, D_MODEL), F32),
        compiler_params=_cparams(("arbitrary",)),
        name="post",
    )(oa, od, x2d, g_moba, g_diff, wo_bf16)


def _ffn_kernel(x_ref, gf_ref, wg_ref, wu_ref, cw_ref, cb_ref, wd_ref, e1_ref, e2_ref,
                o_ref, g_out_ref, gbuf_ref, carry_ref, *, tm, seq_len):
    long_seq = seq_len >= tm
    x = x_ref[0]
    h = _rms(x, gf_ref[...]).astype(BF16)
    if long_seq:
        @pl.when(pl.program_id(1) == 0)
        def _():
            carry_ref[...] = jnp.zeros(carry_ref.shape, F32)
    else:
        tmod = lax.broadcasted_iota(jnp.int32, (tm, 1), 0) % seq_len
    g_rows = g_out_ref.shape[1]
    y = x
    for fc in range(N_FF_CHUNKS):
        cs = slice(fc * FF_CHUNK, (fc + 1) * FF_CHUNK)
        g = _dot(h, wg_ref[:, cs])
        u = _dot(h, wu_ref[:, cs])
        gbuf_ref[SUBLANES:SUBLANES + tm, :] = g
        if long_seq:
            gbuf_ref[0:SUBLANES, :] = carry_ref[:, cs]
            carry_ref[:, cs] = g[tm - SUBLANES:tm]
        else:
            gbuf_ref[0:SUBLANES, :] = jnp.zeros((SUBLANES, FF_CHUNK), F32)
        gm1 = gbuf_ref[SUBLANES - 1:SUBLANES - 1 + tm, :]
        gm2 = gbuf_ref[SUBLANES - 2:SUBLANES - 2 + tm, :]
        if not long_seq:
            gm1 = jnp.where(tmod >= 1, gm1, e1_ref[0, :, cs])
            gm2 = jnp.where(tmod >= 2, gm2, e2_ref[0, :, cs])
        cw = cw_ref[:, cs]
        gc = cw[0:1] * gm2 + cw[1:2] * gm1 + cw[2:3] * g + cb_ref[:, cs]
        act = (jax.nn.gelu(gc) * u).astype(BF16)
        y = y + _dot(act, wd_ref[cs, :])
        g_out_ref[0, :, cs] = g[tm - g_rows:tm]
    o_ref[0] = y


def _ffn(x3d, g_ffn, wg, wu, conv_w, conv_b, wd, e1, e2, tm, seq_len):
    nb, s, _ = x3d.shape
    n_tiles = s // tm
    const = lambda shape: pl.BlockSpec(shape, lambda bi, i: (0,) * len(shape), pipeline_mode=pl.Buffered(1))
    e_rows = e1.shape[1]
    g_rows = SUBLANES if seq_len >= tm else tm
    return pl.pallas_call(
        functools.partial(_ffn_kernel, tm=tm, seq_len=seq_len),
        grid=(nb, n_tiles),
        in_specs=[pl.BlockSpec((1, tm, D_MODEL), lambda bi, i: (bi, i, 0)),
                  const((1, D_MODEL)), const((D_MODEL, D_FF)), const((D_MODEL, D_FF)),
                  const((3, D_FF)), const((1, D_FF)), const((D_FF, D_MODEL)),
                  pl.BlockSpec((1, e_rows, D_FF), lambda bi, i: (bi, 0, 0)),
                  pl.BlockSpec((1, e_rows, D_FF), lambda bi, i: (bi, 0, 0))],
        out_specs=[pl.BlockSpec((1, tm, D_MODEL), lambda bi, i: (bi, i, 0)),
                   pl.BlockSpec((1, g_rows, D_FF), lambda bi, i: (bi * n_tiles + i, 0, 0))],
        out_shape=[jax.ShapeDtypeStruct((nb, s, D_MODEL), F32),
                   jax.ShapeDtypeStruct((nb * n_tiles, g_rows, D_FF), F32)],
        scratch_shapes=[pltpu.VMEM((tm + SUBLANES, FF_CHUNK), F32), pltpu.VMEM((SUBLANES, D_FF), F32)],
        compiler_params=_cparams(("arbitrary", "arbitrary")),
        name="ffn",
    )(x3d, g_ffn, wg, wu, conv_w, conv_b, wd, e1, e2)


def _ple_kernel(x_ref, p_ref, gp_ref, wpg_ref, wpp_ref, gfin_ref, o_ref):
    x = x_ref[...]
    gate = jax.nn.sigmoid(_dot(_rms(x, gp_ref[...]).astype(BF16), wpg_ref[...]))
    x = x + _dot(p_ref[...].astype(BF16), wpp_ref[...]) * gate
    o_ref[...] = _rms(x, gfin_ref[...])


def _ple(x2d, p2d, g_ple, wpg, wpp, g_final, tm):
    t = x2d.shape[0]
    const = lambda shape: pl.BlockSpec(shape, lambda i: (0,) * len(shape))
    return pl.pallas_call(
        _ple_kernel,
        grid=(t // tm,),
        in_specs=[pl.BlockSpec((tm, D_MODEL), lambda i: (i, 0)),
                  pl.BlockSpec((tm, PLE_DIM), lambda i: (i, 0)),
                  const((1, D_MODEL)), const((D_MODEL, D_MODEL)), const((PLE_DIM, D_MODEL)),
                  const((1, D_MODEL))],
        out_specs=pl.BlockSpec((tm, D_MODEL), lambda i: (i, 0)),
        out_shape=jax.ShapeDtypeStruct((t, D_MODEL), F32),
        compiler_params=_cparams(("arbitrary",)),
        name="ple",
    )(x2d, p2d, g_ple, wpg, wpp, g_final)


def kernel(x_prompt, x_sample, cache_k_moba, cache_v_moba, cache_k_diff, cache_v_diff, state_conv, page_table,
           p_prompt, p_sample, g_attn, w_in, lam_params, g_moba_out, g_diff_sub, w_o, g_ffn, w_gate, w_up,
           conv_w, conv_b, w_down, g_ple, w_ple_gate, w_ple_proj, g_final):
    assert w_in.shape[0] == 1, "single-layer trunk"
    bp, sp, _ = x_prompt.shape
    bs, ss, _ = x_sample.shape
    assert sp >= 2 and ss >= 2, "the returned conv state is cut from this step's own gate rows"
    n_pool = cache_k_moba.shape[1]

    row = lambda v: v.reshape(1, -1)
    w_in_b = w_in[0].astype(BF16)
    col = lambda k: w_in_b[:, k * GROUP_WIDTH:(k + 1) * GROUP_WIDTH]
    w_o_b = w_o[0].astype(BF16)
    w_gate_b = w_gate[0].astype(BF16)
    w_up_b = w_up[0].astype(BF16)
    w_down_b = w_down[0].astype(BF16)
    w_pg_b = w_ple_gate[0].astype(BF16)
    w_pp_b = w_ple_proj[0].astype(BF16)
    slopes_a = _alibi_slopes(N_HEADS_MOBA)
    slopes_d = _alibi_slopes(N_HEADS_DIFF)
    lam_p = lam_params[0].astype(F32)

    def tail(oa, od, x3d, p3d, e1, e2, tm, seq_len):
        nb, s, _ = x3d.shape
        t = nb * s
        x1 = _post(oa.reshape(t, GROUP_WIDTH), od.reshape(t, GROUP_WIDTH), x3d.reshape(t, D_MODEL),
                   row(g_moba_out[0]), row(g_diff_sub[0]), w_o_b, min(tm, t))
        x2, g_tail = _ffn(x1.reshape(nb, s, D_MODEL), row(g_ffn[0]), w_gate_b, w_up_b, conv_w[0], row(conv_b[0]),
                          w_down_b, e1, e2, tm, seq_len)
        y = _ple(x2.reshape(t, D_MODEL), p3d.reshape(t, PLE_DIM), row(g_ple[0]), w_pg_b, w_pp_b, row(g_final),
                 min(tm, t))
        return y.reshape(nb, s, D_MODEL), g_tail

    w_rm = jnp.concatenate([col(0), col(3), col(5)], axis=1)
    w_t = jnp.concatenate([col(1), col(2), col(4)], axis=1).T
    qa, qd, vd, kat, vat, kdt = _proj(x_prompt, row(g_attn[0]), w_rm, w_t, 512)
    oa = _moba_prompt(slopes_a, qa, kat, vat)
    od = _diff_prompt(slopes_d, lam_p, qd, kdt, vd)
    tm_p = 512
    dummy = jnp.zeros((bp, SUBLANES, D_FF), F32)
    y_prompt, g_tail_p = tail(oa, od, x_prompt, p_prompt[0], dummy, dummy, tm_p, sp)
    conv_prompt = g_tail_p.reshape(bp, sp // tm_p, SUBLANES, D_FF)[:, -1, SUBLANES - 2:]
    outs_p = (jnp.transpose(kat, (0, 3, 1, 2))[None], jnp.transpose(vat, (0, 3, 1, 2))[None],
              jnp.transpose(kdt.reshape(bp, N_HEADS_DIFF, 2, HEAD_DIM, sp), (0, 4, 1, 2, 3))[None],
              vd.reshape(1, bp, sp, N_HEADS_DIFF, 2 * HEAD_DIM), conv_prompt[None])

    ts = bs * ss
    qa, ka, va, qd, kd, vd = _proj(x_sample.reshape(1, ts, D_MODEL), row(g_attn[0]), w_in_b, None, ts)
    sh = lambda a: a.reshape(bs, ss, GROUP_WIDTH)
    kt_pages = lambda cch: jnp.moveaxis(cch[0].reshape(n_pool, PAGE_SIZE, N_KT_HEADS, HEAD_DIM), 1, 3)
    oa = _moba_sample(page_table, slopes_a, sh(qa), sh(ka), sh(va), kt_pages(cache_k_moba), kt_pages(cache_v_moba))
    od = _diff_sample(page_table, slopes_d, lam_p, sh(qd), sh(kd), sh(vd), kt_pages(cache_k_diff), cache_v_diff[0])
    prev = state_conv[0]
    zero = jnp.zeros((bs, ss - 1, D_FF), F32)
    e1 = jnp.concatenate([prev[:, 1:2], zero], axis=1).reshape(1, ts, D_FF)
    e2 = jnp.concatenate([prev, zero[:, 1:]], axis=1).reshape(1, ts, D_FF)
    y_sample, g_tail_s = tail(oa.reshape(1, ts, GROUP_WIDTH), od.reshape(1, ts, GROUP_WIDTH),
                              x_sample.reshape(1, ts, D_MODEL), p_sample[0].reshape(1, ts, PLE_DIM), e1, e2, ts, ss)
    conv_sample = g_tail_s.reshape(bs, ss, D_FF)[:, ss - 2:]
    outs_s = (ka.reshape(1, bs, ss, N_HEADS_MOBA, HEAD_DIM), va.reshape(1, bs, ss, N_HEADS_MOBA, HEAD_DIM),
              kd.reshape(1, bs, ss, N_HEADS_DIFF, 2, HEAD_DIM), vd.reshape(1, bs, ss, N_HEADS_DIFF, 2 * HEAD_DIM),
              conv_sample[None])
    return (y_prompt, y_sample.reshape(bs, ss, D_MODEL)) + outs_p + outs_s
```

```python
import functools
import math

import jax
import jax.numpy as jnp
import numpy as np
from jax import lax
from jax.experimental import pallas as pl
from jax.experimental.pallas import tpu as pltpu

F32 = jnp.float32
BF16 = jnp.bfloat16

D_MODEL = 1024
HEAD_DIM = 64
N_HEADS_MOBA = 8
N_HEADS_DIFF = 4
N_KT_HEADS = 8
GROUP_WIDTH = 512
MOBA_BLOCK = 256
MOBA_TOPK = 3
PAGE_SIZE = 128
D_FF = 2816
FF_CHUNK = 256
N_FF_CHUNKS = D_FF // FF_CHUNK
PLE_DIM = 256
RMS_EPS = 1e-6
LAM_INIT = 0.8 - 0.6 * math.exp(-0.3 * 0)
QK_SCALE = HEAD_DIM ** -0.5
NEG_INF = float("-inf")
MASKED = -1e30

LANES = 128
SUBLANES = 8
VMEM_LIMIT = 56 * 1024 * 1024
PAGES_PER_STEP = 32
CHUNK_KEYS = PAGES_PER_STEP * PAGE_SIZE
HIGHEST = lax.Precision.HIGHEST

FEAT_QROW = 0
FEAT_QBLK = 1
FEAT_KROW = 2
FEAT_KBLK = 3
FEAT_MASK0 = SUBLANES


def _cparams(sem):
    return pltpu.CompilerParams(dimension_semantics=sem, vmem_limit_bytes=VMEM_LIMIT)


def _rms(x, g):
    return (x * lax.rsqrt(jnp.mean(x * x, axis=-1, keepdims=True) + RMS_EPS)) * g


def _dot_nt(a, b, precision=None):
    return lax.dot_general(a, b, (((1,), (1,)), ((), ())), precision=precision,
                           preferred_element_type=F32)


def _dot(a, b, precision=None):
    return jnp.dot(a, b, precision=precision, preferred_element_type=F32)


def _alibi_slopes(n_heads):
    return jnp.asarray(2.0 ** (-8.0 * np.arange(1, n_heads + 1) / n_heads), dtype=F32)


def _top3_select(gate, valid, axis):
    n = gate.shape[axis]
    idx = lax.broadcasted_iota(jnp.int32, gate.shape, axis)
    g = jnp.where(valid, gate, NEG_INF)
    sel = jnp.zeros(gate.shape, jnp.bool_)
    for _ in range(MOBA_TOPK):
        m = jnp.max(g, axis=axis, keepdims=True)
        first = jnp.min(jnp.where(g == m, idx, n), axis=axis, keepdims=True)
        pick = (idx == first) & (m > NEG_INF)
        sel = sel | (pick & (m < float("inf")))
        g = jnp.where(pick, NEG_INF, g)
    return sel


def _proj_rows_kernel(x_ref, g_ref, w_ref, *out_refs):
    h = _rms(x_ref[0], g_ref[...]).astype(BF16)
    for idx, o_ref in enumerate(out_refs):
        o_ref[0] = _dot(h, w_ref[:, idx * GROUP_WIDTH:(idx + 1) * GROUP_WIDTH])


def _proj_rows(x3d, g, w, tm):
    nb, s, _ = x3d.shape
    n_out = w.shape[1] // GROUP_WIDTH
    const = lambda shape: pl.BlockSpec(shape, lambda bi, i: (0,) * len(shape))
    return pl.pallas_call(
        _proj_rows_kernel,
        grid=(nb, s // tm),
        in_specs=[pl.BlockSpec((1, tm, D_MODEL), lambda bi, i: (bi, i, 0)), const((1, D_MODEL)), const(w.shape)],
        out_specs=[pl.BlockSpec((1, tm, GROUP_WIDTH), lambda bi, i: (bi, i, 0))] * n_out,
        out_shape=[jax.ShapeDtypeStruct((nb, s, GROUP_WIDTH), F32)] * n_out,
        compiler_params=_cparams(("arbitrary", "arbitrary")),
        name="proj_rows",
    )(x3d, g, w)


def _proj_prompt_kernel(x_ref, g_ref, wr_ref, wt_ref, qa_ref, qd_ref, vd_ref, kab_ref, kdb_ref, mean_ref,
                        kat_ref, vat_ref, kdt_ref, vatb_ref, vdtb_ref):
    h = _rms(x_ref[0], g_ref[...]).astype(BF16)
    tm = h.shape[0]
    col = lambda k: wr_ref[:, k * GROUP_WIDTH:(k + 1) * GROUP_WIDTH]
    rows = lambda k: wt_ref[k * GROUP_WIDTH:(k + 1) * GROUP_WIDTH, :]
    qa_ref[0] = _dot(h, col(0))
    qd_ref[0] = _dot(h, col(1))
    vd_ref[0] = _dot(h, col(2))
    ka = _dot(h, col(3))
    kab_ref[0] = ka.astype(BF16)
    for r in range(tm // MOBA_BLOCK):
        mean_ref[0, r] = jnp.mean(ka[r * MOBA_BLOCK:(r + 1) * MOBA_BLOCK], axis=0, keepdims=True)
    kdb_ref[0] = _dot(h, col(4)).astype(BF16)
    kat_ref[0] = _dot_nt(rows(0), h).reshape(N_KT_HEADS, HEAD_DIM, tm)
    vat = _dot_nt(rows(1), h).reshape(N_KT_HEADS, HEAD_DIM, tm)
    vat_ref[0] = vat
    vatb_ref[0] = vat.astype(BF16)
    kdt_ref[0] = _dot_nt(rows(2), h).reshape(N_KT_HEADS, HEAD_DIM, tm)
    vdtb_ref[0] = _dot_nt(rows(3), h).astype(BF16).reshape(N_HEADS_DIFF, 2 * HEAD_DIM, tm)


def _proj_prompt(x3d, g, wr, wt, tm):
    nb, s, _ = x3d.shape
    const = lambda shape: pl.BlockSpec(shape, lambda bi, i: (0,) * len(shape))
    rm_spec = pl.BlockSpec((1, tm, GROUP_WIDTH), lambda bi, i: (bi, i, 0))
    kt_spec = pl.BlockSpec((1, N_KT_HEADS, HEAD_DIM, tm), lambda bi, i: (bi, 0, 0, i))
    rm = lambda dt: jax.ShapeDtypeStruct((nb, s, GROUP_WIDTH), dt)
    kt = lambda dt: jax.ShapeDtypeStruct((nb, N_KT_HEADS, HEAD_DIM, s), dt)
    return pl.pallas_call(
        _proj_prompt_kernel,
        grid=(nb, s // tm),
        in_specs=[pl.BlockSpec((1, tm, D_MODEL), lambda bi, i: (bi, i, 0)), const((1, D_MODEL)),
                  const(wr.shape), const(wt.shape)],
        out_specs=[rm_spec] * 5
        + [pl.BlockSpec((1, tm // MOBA_BLOCK, 1, GROUP_WIDTH), lambda bi, i: (bi, i, 0, 0))]
        + [kt_spec] * 4
        + [pl.BlockSpec((1, N_HEADS_DIFF, 2 * HEAD_DIM, tm), lambda bi, i: (bi, 0, 0, i))],
        out_shape=[rm(F32), rm(F32), rm(F32), rm(BF16), rm(BF16),
                   jax.ShapeDtypeStruct((nb, s // MOBA_BLOCK, 1, GROUP_WIDTH), F32),
                   kt(F32), kt(F32), kt(F32), kt(BF16),
                   jax.ShapeDtypeStruct((nb, N_HEADS_DIFF, 2 * HEAD_DIM, s), BF16)],
        compiler_params=_cparams(("arbitrary", "arbitrary")),
        name="proj_prompt",
    )(x3d, g, wr, wt)


def _build_key_features(k_ref, kaug_ref, slopes, n_blocks):
    lane = lax.broadcasted_iota(jnp.int32, (MOBA_BLOCK, LANES), 1)
    krow = lax.broadcasted_iota(jnp.int32, (MOBA_BLOCK, LANES), 0).astype(F32)
    for mp in range(2):
        f = lane - HEAD_DIM * (1 - mp)
        own_half = (lane >= HEAD_DIM * mp) & (lane < HEAD_DIM * (mp + 1))
        slope = slopes[mp]
        for j in range(n_blocks):
            feats = jnp.where((f == FEAT_QROW) | (f == FEAT_QBLK) | (f == FEAT_MASK0 + j), 1.0,
                              jnp.where(f == FEAT_KROW, slope * krow,
                                        jnp.where(f == FEAT_KBLK, slope * float(MOBA_BLOCK * j), 0.0)))
            rs = slice(j * MOBA_BLOCK, (j + 1) * MOBA_BLOCK)
            kaug_ref[mp, rs, :] = jnp.where(own_half, k_ref[0, rs, :], feats.astype(BF16))


def _query_features(q_ref, slopes, i, masks):
    tq = q_ref.shape[1]
    qt = (q_ref[0] * QK_SCALE).T
    frow = lax.broadcasted_iota(jnp.int32, (SUBLANES, tq), 0)
    qrow = lax.broadcasted_iota(jnp.int32, (SUBLANES, tq), 1).astype(F32)
    out = []
    for mp in range(2):
        slope = slopes[mp]
        head = jnp.where(frow == FEAT_QROW, -slope * qrow,
                         jnp.where(frow == FEAT_QBLK, -slope * (i * MOBA_BLOCK).astype(F32),
                                   jnp.where((frow == FEAT_KROW) | (frow == FEAT_KBLK), 1.0, 0.0)))
        pieces = [head]
        used = SUBLANES
        if masks is not None:
            pieces.append(masks[mp])
            used += masks[mp].shape[0]
        pieces.append(jnp.zeros((HEAD_DIM - used, tq), F32))
        feats = jnp.concatenate(pieces, axis=0)
        own = qt[mp * HEAD_DIM:(mp + 1) * HEAD_DIM]
        out.append(jnp.concatenate([own, feats] if mp == 0 else [feats, own], axis=0).astype(BF16))
    return out


def _softmax_step(state, s_tiles, pv):
    m_i, l_i, acc = state
    m_new = m_i
    for s in s_tiles:
        m_new = jnp.maximum(m_new, jnp.max(s, axis=0, keepdims=True))
    alpha = jnp.exp(m_i - m_new)
    p_tiles = [jnp.exp(s - m_new) for s in s_tiles]
    l_new = alpha * l_i
    for p in p_tiles:
        l_new = l_new + jnp.sum(p, axis=0, keepdims=True)
    return m_new, l_new, alpha * acc + pv([p.astype(BF16) for p in p_tiles])


def _attend_blocks(i, kaug_ref, qaug, pv_fns, widths):
    tq = qaug[0].shape[1]
    kblock = lambda mp, j: kaug_ref[mp, pl.ds(pl.multiple_of(j * MOBA_BLOCK, MOBA_BLOCK), MOBA_BLOCK), :]
    krow = lax.broadcasted_iota(jnp.int32, (MOBA_BLOCK, tq), 0)
    qcol = lax.broadcasted_iota(jnp.int32, (MOBA_BLOCK, tq), 1)
    causal = krow <= qcol
    j_prev = jnp.maximum(i - 1, 0)
    prev_off = jnp.where(i % 2 == 1, 0.0, MASKED).astype(F32)
    n_pairs = i // 2

    def pair_scores(j2):
        return tuple((_dot(kblock(mp, 2 * j2), qaug[mp]), _dot(kblock(mp, 2 * j2 + 1), qaug[mp]))
                     for mp in range(2))

    s_first = [(_dot(kblock(mp, j_prev), qaug[mp]) + prev_off,
                jnp.where(causal, _dot(kblock(mp, i), qaug[mp]), MASKED)) for mp in range(2)]
    s_next = pair_scores(0)
    states = []
    for mp in range(2):
        init = (jnp.full((1, tq), NEG_INF, F32), jnp.zeros((1, tq), F32), jnp.zeros((widths[mp], tq), F32))
        states.append(_softmax_step(init, list(s_first[mp]), functools.partial(pv_fns[mp], j_prev, i)))

    def body(j2, carry):
        states, s_cur = carry
        s_next = pair_scores(jnp.minimum(j2 + 1, jnp.maximum(n_pairs - 1, 0)))
        states = tuple(
            _softmax_step(states[mp], list(s_cur[mp]), functools.partial(pv_fns[mp], 2 * j2, 2 * j2 + 1))
            for mp in range(2))
        return states, s_next

    states, _ = lax.fori_loop(0, n_pairs, body, (tuple(states), s_next))
    return [(st[1], st[2]) for st in states]


def _moba_prompt_kernel(slopes_ref, q_ref, mean_ref, k_ref, vt_ref, o_ref, kaug_ref, vtb_ref, *, n_blocks):
    hp = pl.program_id(1)
    i = pl.program_id(2)
    tq = MOBA_BLOCK
    slopes = [slopes_ref[hp * 2], slopes_ref[hp * 2 + 1]]

    @pl.when(i == 0)
    def _():
        _build_key_features(k_ref, kaug_ref, slopes, n_blocks)
        for hh in range(2):
            for j in range(n_blocks):
                vtb_ref[hh * n_blocks + j] = vt_ref[0, hh, :, j * MOBA_BLOCK:(j + 1) * MOBA_BLOCK]

    lane = lax.broadcasted_iota(jnp.int32, (n_blocks, LANES), 1)
    blk = lax.broadcasted_iota(jnp.int32, (n_blocks, tq), 0)
    means = mean_ref[0]
    masks = []
    for hh in range(2):
        own_half = (lane >= HEAD_DIM * hh) & (lane < HEAD_DIM * (hh + 1))
        gate_t = _dot_nt(jnp.where(own_half, means, 0.0), q_ref[0], precision=HIGHEST)
        sel = _top3_select(gate_t, blk < i, axis=0)
        masks.append(jnp.where(sel | (blk == i), 0.0, MASKED))
    qaug = _query_features(q_ref, slopes, i, masks)

    def pv(hh, ja, jb, p_tiles):
        return _dot(vtb_ref[hh * n_blocks + ja], p_tiles[0]) + _dot(vtb_ref[hh * n_blocks + jb], p_tiles[1])

    res = _attend_blocks(i, kaug_ref, qaug, [functools.partial(pv, 0), functools.partial(pv, 1)],
                         [HEAD_DIM, HEAD_DIM])
    out_t = jnp.concatenate([acc / l_i for l_i, acc in res], axis=0)
    o_ref[0] = out_t.T


def _moba_prompt(slopes, qa, means, ka_b, vat_b):
    b, s, _ = qa.shape
    n_blocks = s // MOBA_BLOCK
    blk_spec = pl.BlockSpec((1, MOBA_BLOCK, LANES), lambda bi, hp, i, sl: (bi, i, hp))
    return pl.pallas_call(
        functools.partial(_moba_prompt_kernel, n_blocks=n_blocks),
        grid_spec=pltpu.PrefetchScalarGridSpec(
            num_scalar_prefetch=1,
            grid=(b, N_HEADS_MOBA // 2, n_blocks),
            in_specs=[blk_spec,
                      pl.BlockSpec((1, n_blocks, LANES), lambda bi, hp, i, sl: (bi, 0, hp)),
                      pl.BlockSpec((1, s, LANES), lambda bi, hp, i, sl: (bi, 0, hp)),
                      pl.BlockSpec((1, 2, HEAD_DIM, s), lambda bi, hp, i, sl: (bi, hp, 0, 0))],
            out_specs=blk_spec,
            scratch_shapes=[pltpu.VMEM((2, s, LANES), BF16),
                            pltpu.VMEM((2 * n_blocks, HEAD_DIM, MOBA_BLOCK), BF16)]),
        out_shape=jax.ShapeDtypeStruct((b, s, GROUP_WIDTH), F32),
        compiler_params=_cparams(("arbitrary", "arbitrary", "arbitrary")),
        name="moba_prompt",
    )(slopes, qa, means, ka_b, vat_b)


def _lambda(lp_ref):
    lp = lp_ref[...]
    a = jnp.sum(lp[0:1] * lp[1:2], axis=1, keepdims=True)
    c = jnp.sum(lp[2:3] * lp[3:4], axis=1, keepdims=True)
    return jnp.exp(a) - jnp.exp(c) + LAM_INIT


def _diff_prompt_kernel(slopes_ref, lp_ref, q_ref, k_ref, vt_ref, o_ref, kaug_ref, vtb_ref, *, n_blocks):
    h = pl.program_id(1)
    i = pl.program_id(2)
    slopes = [slopes_ref[h], slopes_ref[h]]

    @pl.when(i == 0)
    def _():
        _build_key_features(k_ref, kaug_ref, slopes, n_blocks)
        for j in range(n_blocks):
            vtb_ref[j] = vt_ref[0, 0, :, j * MOBA_BLOCK:(j + 1) * MOBA_BLOCK]

    qaug = _query_features(q_ref, slopes, i, None)

    def pv(ja, jb, p_tiles):
        return _dot(vtb_ref[ja], p_tiles[0]) + _dot(vtb_ref[jb], p_tiles[1])

    (l1, acc1), (l2, acc2) = _attend_blocks(i, kaug_ref, qaug, [pv, pv], [2 * HEAD_DIM, 2 * HEAD_DIM])
    out_t = acc1 / l1 - _lambda(lp_ref) * (acc2 / l2)
    o_ref[0] = out_t.T


def _diff_prompt(slopes, lam_params, qd, kd_b, vdt_b):
    b, s, _ = qd.shape
    n_blocks = s // MOBA_BLOCK
    blk_spec = pl.BlockSpec((1, MOBA_BLOCK, LANES), lambda bi, h, i, sl: (bi, i, h))
    return pl.pallas_call(
        functools.partial(_diff_prompt_kernel, n_blocks=n_blocks),
        grid_spec=pltpu.PrefetchScalarGridSpec(
            num_scalar_prefetch=1,
            grid=(b, N_HEADS_DIFF, n_blocks),
            in_specs=[pl.BlockSpec((4, HEAD_DIM), lambda bi, h, i, sl: (0, 0)), blk_spec,
                      pl.BlockSpec((1, s, LANES), lambda bi, h, i, sl: (bi, 0, h)),
                      pl.BlockSpec((1, 1, 2 * HEAD_DIM, s), lambda bi, h, i, sl: (bi, h, 0, 0))],
            out_specs=blk_spec,
            scratch_shapes=[pltpu.VMEM((2, s, LANES), BF16),
                            pltpu.VMEM((n_blocks, 2 * HEAD_DIM, MOBA_BLOCK), BF16)]),
        out_shape=jax.ShapeDtypeStruct((b, s, GROUP_WIDTH), F32),
        compiler_params=_cparams(("arbitrary", "arbitrary", "arbitrary")),
        name="diff_prompt",
    )(slopes, lam_params, qd, kd_b, vdt_b)


def _block_diag_q(q4):
    n_q = q4.shape[0]
    rows = N_KT_HEADS * n_q
    tiled = jnp.concatenate([q4] * N_KT_HEADS, axis=0)
    r = lax.broadcasted_iota(jnp.int32, (rows, GROUP_WIDTH), 0)
    c = lax.broadcasted_iota(jnp.int32, (rows, GROUP_WIDTH), 1)
    return jnp.where(r // n_q == c // HEAD_DIM, tiled, 0.0)


def _pad_rows(x, rows):
    return jnp.concatenate([x, jnp.zeros((rows - x.shape[0], x.shape[1]), x.dtype)], axis=0)


def _row_slopes(slopes_ref, head_of_row, n_heads):
    slope = jnp.zeros(head_of_row.shape, F32)
    for hh in range(n_heads):
        slope = jnp.where(head_of_row == hh, slopes_ref[hh], slope)
    return slope


def _new_key_scores(qb, kn_ref, slope, t, n_q):
    kn = _pad_rows(kn_ref[0], SUBLANES)
    sn = _dot_nt(qb, kn.astype(BF16))
    u = lax.broadcasted_iota(jnp.int32, sn.shape, 1)
    return jnp.where((u <= t) & (u < n_q), sn - slope * (t - u).astype(F32), NEG_INF)


def _moba_sample_kernel(pt_ref, slopes_ref, q_ref, kn_ref, vn_ref, *rest, n_q, n_chunks):
    k_pages = rest[:PAGES_PER_STEP]
    v_pages = rest[PAGES_PER_STEP:2 * PAGES_PER_STEP]
    o_ref = rest[2 * PAGES_PER_STEP]
    s_ref, p_ref, gate_ref, acc_ref, l_ref, pn_ref = rest[2 * PAGES_PER_STEP + 1:]
    c = pl.program_id(1)
    rows = N_HEADS_MOBA * n_q
    n_blocks = n_chunks * CHUNK_KEYS // MOBA_BLOCK
    blocks_per_step = CHUNK_KEYS // MOBA_BLOCK
    past_len = n_chunks * CHUNK_KEYS
    qb = (_block_diag_q(q_ref[0]) * QK_SCALE).astype(BF16)

    @pl.when(c == 0)
    def _():
        gate_ref[...] = jnp.zeros(gate_ref.shape, F32)

    @pl.when(c < n_chunks)
    def _():
        blk_lane = lax.broadcasted_iota(jnp.int32, (rows, n_blocks), 1)
        gate = gate_ref[...]
        for p in range(PAGES_PER_STEP):
            kp = k_pages[p][0].reshape(GROUP_WIDTH, PAGE_SIZE)
            sp = _dot(qb, kp.astype(BF16))
            s_ref[:, pl.ds(pl.multiple_of(c * CHUNK_KEYS + p * PAGE_SIZE, PAGE_SIZE), PAGE_SIZE)] = sp
            if p % 2 == 0:
                pair = sp
            else:
                bsum = jnp.sum(pair + sp, axis=1, keepdims=True) / MOBA_BLOCK
                gate = jnp.where(blk_lane == c * blocks_per_step + p // 2, bsum, gate)
        gate_ref[...] = gate

    @pl.when(c == n_chunks - 1)
    def _():
        gate = gate_ref[...]
        sel = _top3_select(gate, jnp.ones(gate.shape, jnp.bool_), axis=1)
        eb = lax.broadcasted_iota(jnp.int32, (n_blocks, past_len), 0)
        ek = lax.broadcasted_iota(jnp.int32, (n_blocks, past_len), 1)
        expand = jnp.where(ek // MOBA_BLOCK == eb, 1.0, 0.0).astype(BF16)
        selk = _dot(jnp.where(sel, 1.0, 0.0).astype(BF16), expand)
        r = lax.broadcasted_iota(jnp.int32, (rows, 1), 0)
        t = r % n_q
        slope = _row_slopes(slopes_ref, r // n_q, N_HEADS_MOBA)
        kpos = lax.broadcasted_iota(jnp.int32, (rows, past_len), 1)
        s = jnp.where(selk > 0.5, s_ref[...] - slope * (past_len + t - kpos).astype(F32), NEG_INF)
        sn = _new_key_scores(qb, kn_ref, slope, t, n_q)
        m = jnp.maximum(jnp.max(s, axis=1, keepdims=True), jnp.max(sn, axis=1, keepdims=True))
        p = jnp.exp(s - m)
        pn = jnp.exp(sn - m)
        l_ref[...] = jnp.sum(p, axis=1, keepdims=True) + jnp.sum(pn, axis=1, keepdims=True)
        p_ref[...] = p.astype(BF16)
        pn_ref[...] = pn
        acc_ref[...] = jnp.zeros(acc_ref.shape, F32)

    @pl.when(c >= n_chunks)
    def _():
        cv = c - n_chunks
        acc = acc_ref[...]
        for p in range(PAGES_PER_STEP):
            pp = p_ref[:, pl.ds(pl.multiple_of(cv * CHUNK_KEYS + p * PAGE_SIZE, PAGE_SIZE), PAGE_SIZE)]
            vp = v_pages[p][0].reshape(GROUP_WIDTH, PAGE_SIZE)
            acc = acc + _dot_nt(pp, vp.astype(BF16))
        acc_ref[...] = acc

    @pl.when(c == 2 * n_chunks - 1)
    def _():
        vn = _pad_rows(vn_ref[0], SUBLANES)
        acc = acc_ref[...] + _dot(pn_ref[...].astype(BF16), vn.astype(BF16))
        acc = acc / l_ref[...]
        r = lax.broadcasted_iota(jnp.int32, (rows, GROUP_WIDTH), 0)
        cc = lax.broadcasted_iota(jnp.int32, (rows, GROUP_WIDTH), 1)
        acc = jnp.where(r // n_q == cc // HEAD_DIM, acc, 0.0)
        out = acc[0:n_q]
        for hh in range(1, N_HEADS_MOBA):
            out = out + acc[hh * n_q:(hh + 1) * n_q]
        o_ref[0] = out


def _page_specs(block_shape, n_chunks, phase_offset):
    zeros = (0,) * (len(block_shape) - 1)

    def make(p):
        def index_map(bi, c, pt, sl):
            chunk = jnp.clip(c - phase_offset, 0, n_chunks - 1)
            return (pt[bi, chunk * PAGES_PER_STEP + p],) + zeros
        return pl.BlockSpec(block_shape, index_map)
    return [make(p) for p in range(PAGES_PER_STEP)]


KT_PAGE = (1, N_KT_HEADS, HEAD_DIM, PAGE_SIZE)


def _moba_sample(page_table, slopes, qa, ka_new, va_new, cache_kt, cache_vt):
    b, n_q, _ = qa.shape
    n_pages = page_table.shape[1]
    n_chunks = n_pages // PAGES_PER_STEP
    past_len = n_pages * PAGE_SIZE
    rows = N_HEADS_MOBA * n_q
    tok_spec = pl.BlockSpec((1, n_q, GROUP_WIDTH), lambda bi, c, pt, sl: (bi, 0, 0))
    return pl.pallas_call(
        functools.partial(_moba_sample_kernel, n_q=n_q, n_chunks=n_chunks),
        grid_spec=pltpu.PrefetchScalarGridSpec(
            num_scalar_prefetch=2,
            grid=(b, 2 * n_chunks),
            in_specs=[tok_spec, tok_spec, tok_spec] + _page_specs(KT_PAGE, n_chunks, 0)
            + _page_specs(KT_PAGE, n_chunks, n_chunks),
            out_specs=tok_spec,
            scratch_shapes=[pltpu.VMEM((rows, past_len), F32), pltpu.VMEM((rows, past_len), BF16),
                            pltpu.VMEM((rows, past_len // MOBA_BLOCK), F32),
                            pltpu.VMEM((rows, GROUP_WIDTH), F32), pltpu.VMEM((rows, 1), F32),
                            pltpu.VMEM((rows, SUBLANES), F32)]),
        out_shape=jax.ShapeDtypeStruct((b, n_q, GROUP_WIDTH), F32),
        compiler_params=_cparams(("arbitrary", "arbitrary")),
        name="moba_sample",
    )(page_table, slopes, qa, ka_new, va_new, *([cache_kt] * PAGES_PER_STEP), *([cache_vt] * PAGES_PER_STEP))


def _diff_sample_kernel(pt_ref, slopes_ref, lp_ref, q_ref, kn_ref, vn_ref, *rest, n_q, n_chunks):
    k_pages = rest[:PAGES_PER_STEP]
    v_pages = rest[PAGES_PER_STEP:2 * PAGES_PER_STEP]
    o_ref = rest[2 * PAGES_PER_STEP]
    s_ref, m_ref, l_ref, acc_ref = rest[2 * PAGES_PER_STEP + 1:]
    c = pl.program_id(1)
    rows = N_KT_HEADS * n_q
    head_rows = 2 * n_q
    e = 2 * HEAD_DIM
    past_len = n_chunks * CHUNK_KEYS
    qb = (_block_diag_q(q_ref[0]) * QK_SCALE).astype(BF16)

    @pl.when(c == 0)
    def _():
        m_ref[...] = jnp.full(m_ref.shape, NEG_INF, F32)
        l_ref[...] = jnp.zeros(l_ref.shape, F32)
        acc_ref[...] = jnp.zeros(acc_ref.shape, F32)

    r = lax.broadcasted_iota(jnp.int32, (rows, 1), 0)
    t = r % n_q
    slope = _row_slopes(slopes_ref, r // head_rows, N_HEADS_DIFF)

    for p in range(PAGES_PER_STEP):
        kp = k_pages[p][0].reshape(GROUP_WIDTH, PAGE_SIZE)
        s_ref[:, p * PAGE_SIZE:(p + 1) * PAGE_SIZE] = _dot(qb, kp.astype(BF16))
    kpos = c * CHUNK_KEYS + lax.broadcasted_iota(jnp.int32, (rows, CHUNK_KEYS), 1)
    s = s_ref[...] - slope * (past_len + t - kpos).astype(F32)
    m_old = m_ref[...]
    m_new = jnp.maximum(m_old, jnp.max(s, axis=1, keepdims=True))
    alpha = jnp.exp(m_old - m_new)
    pexp = jnp.exp(s - m_new)
    l_ref[...] = alpha * l_ref[...] + jnp.sum(pexp, axis=1, keepdims=True)
    m_ref[...] = m_new
    pb = pexp.astype(BF16)
    acc = alpha * acc_ref[...]
    parts = []
    for hh in range(N_HEADS_DIFF):
        a = acc[hh * head_rows:(hh + 1) * head_rows]
        for p in range(PAGES_PER_STEP):
            vh = v_pages[p][0, pl.ds(hh, PAGE_SIZE, stride=N_HEADS_DIFF), :].astype(BF16)
            a = a + _dot(pb[hh * head_rows:(hh + 1) * head_rows, p * PAGE_SIZE:(p + 1) * PAGE_SIZE], vh)
        parts.append(a)
    acc_ref[...] = jnp.concatenate(parts, axis=0)

    @pl.when(c == n_chunks - 1)
    def _():
        sn = _new_key_scores(qb, kn_ref, slope, t, n_q)
        m_o = m_ref[...]
        m_f = jnp.maximum(m_o, jnp.max(sn, axis=1, keepdims=True))
        al = jnp.exp(m_o - m_f)
        pn = jnp.exp(sn - m_f)
        l_f = al * l_ref[...] + jnp.sum(pn, axis=1, keepdims=True)
        pnb = pn.astype(BF16)
        vn = _pad_rows(vn_ref[0], SUBLANES).astype(BF16)
        a_old = al * acc_ref[...]
        lam = _lambda(lp_ref)
        outs = []
        for hh in range(N_HEADS_DIFF):
            rs = slice(hh * head_rows, (hh + 1) * head_rows)
            a = (a_old[rs] + _dot(pnb[rs], vn[:, hh * e:(hh + 1) * e])) / l_f[rs]
            outs.append(a[0:n_q] - lam * a[n_q:2 * n_q])
        o_ref[0] = jnp.concatenate(outs, axis=1)


def _diff_sample(page_table, slopes, lam_params, qd, kd_new, vd_new, cache_kt, cache_v):
    b, n_q, _ = qd.shape
    n_pages = page_table.shape[1]
    n_chunks = n_pages // PAGES_PER_STEP
    rows = N_KT_HEADS * n_q
    tok_spec = pl.BlockSpec((1, n_q, GROUP_WIDTH), lambda bi, c, pt, sl: (bi, 0, 0))
    v_page = (1, PAGE_SIZE * N_HEADS_DIFF, 2 * HEAD_DIM)
    return pl.pallas_call(
        functools.partial(_diff_sample_kernel, n_q=n_q, n_chunks=n_chunks),
        grid_spec=pltpu.PrefetchScalarGridSpec(
            num_scalar_prefetch=2,
            grid=(b, n_chunks),
            in_specs=[pl.BlockSpec((4, HEAD_DIM), lambda bi, c, pt, sl: (0, 0)), tok_spec, tok_spec, tok_spec]
            + _page_specs(KT_PAGE, n_chunks, 0) + _page_specs(v_page, n_chunks, 0),
            out_specs=tok_spec,
            scratch_shapes=[pltpu.VMEM((rows, CHUNK_KEYS), F32), pltpu.VMEM((rows, 1), F32),
                            pltpu.VMEM((rows, 1), F32), pltpu.VMEM((rows, 2 * HEAD_DIM), F32)]),
        out_shape=jax.ShapeDtypeStruct((b, n_q, GROUP_WIDTH), F32),
        compiler_params=_cparams(("arbitrary", "arbitrary")),
        name="diff_sample",
    )(page_table, slopes, lam_params, qd, kd_new, vd_new,
      *([cache_kt] * PAGES_PER_STEP), *([cache_v] * PAGES_PER_STEP))


def _post_kernel(oa_ref, od_ref, x_ref, gm_ref, gd_ref, wo_ref, o_ref):
    oa = _rms(oa_ref[...], gm_ref[...]).astype(BF16)
    y = x_ref[...] + _dot(oa, wo_ref[0:GROUP_WIDTH, :])
    od = od_ref[...]
    for h in range(N_HEADS_DIFF):
        w = 2 * HEAD_DIM
        odh = (_rms(od[:, h * w:(h + 1) * w], gd_ref[...]) * (1.0 - LAM_INIT)).astype(BF16)
        y = y + _dot(odh, wo_ref[GROUP_WIDTH + h * w:GROUP_WIDTH + (h + 1) * w, :])
    o_ref[...] = y


def _post(oa, od, x2d, g_moba, g_diff, wo_bf16, tm):
    t = x2d.shape[0]
    return pl.pallas_call(
        _post_kernel,
        grid=(t // tm,),
        in_specs=[pl.BlockSpec((tm, GROUP_WIDTH), lambda i: (i, 0)),
                  pl.BlockSpec((tm, GROUP_WIDTH), lambda i: (i, 0)),
                  pl.BlockSpec((tm, D_MODEL), lambda i: (i, 0)),
                  pl.BlockSpec((1, GROUP_WIDTH), lambda i: (0, 0)),
                  pl.BlockSpec((1, 2 * HEAD_DIM), lambda i: (0, 0)),
                  pl.BlockSpec((D_MODEL, D_MODEL), lambda i: (0, 0))],
        out_specs=pl.BlockSpec((tm, D_MODEL), lambda i: (i, 0)),
        out_shape=jax.ShapeDtypeStruct((t, D_MODEL), F32),
        compiler_params=_cparams(("arbitrary",)),
        name="post",
    )(oa, od, x2d, g_moba, g_diff, wo_bf16)


def _ffn_kernel(x_ref, gf_ref, wg_ref, wu_ref, cw_ref, cb_ref, wd_ref, e1_ref, e2_ref,
                o_ref, g_out_ref, gbuf_ref, carry_ref, *, tm, seq_len):
    long_seq = seq_len >= tm
    x = x_ref[0]
    h = _rms(x, gf_ref[...]).astype(BF16)
    if long_seq:
        @pl.when(pl.program_id(1) == 0)
        def _():
            carry_ref[...] = jnp.zeros(carry_ref.shape, F32)
    else:
        tmod = lax.broadcasted_iota(jnp.int32, (tm, 1), 0) % seq_len
    g_rows = g_out_ref.shape[1]
    y = x
    for fc in range(N_FF_CHUNKS):
        cs = slice(fc * FF_CHUNK, (fc + 1) * FF_CHUNK)
        g = _dot(h, wg_ref[:, cs])
        u = _dot(h, wu_ref[:, cs])
        gbuf_ref[SUBLANES:SUBLANES + tm, :] = g
        if long_seq:
            gbuf_ref[0:SUBLANES, :] = carry_ref[:, cs]
            carry_ref[:, cs] = g[tm - SUBLANES:tm]
        else:
            gbuf_ref[0:SUBLANES, :] = jnp.zeros((SUBLANES, FF_CHUNK), F32)
        gm1 = gbuf_ref[SUBLANES - 1:SUBLANES - 1 + tm, :]
        gm2 = gbuf_ref[SUBLANES - 2:SUBLANES - 2 + tm, :]
        if not long_seq:
            gm1 = jnp.where(tmod >= 1, gm1, e1_ref[0, :, cs])
            gm2 = jnp.where(tmod >= 2, gm2, e2_ref[0, :, cs])
        cw = cw_ref[:, cs]
        gc = cw[0:1] * gm2 + cw[1:2] * gm1 + cw[2:3] * g + cb_ref[:, cs]
        act = (jax.nn.gelu(gc) * u).astype(BF16)
        y = y + _dot(act, wd_ref[cs, :])
        g_out_ref[0, :, cs] = g[tm - g_rows:tm]
    o_ref[0] = y


def _ffn(x3d, g_ffn, wg, wu, conv_w, conv_b, wd, e1, e2, tm, seq_len):
    nb, s, _ = x3d.shape
    n_tiles = s // tm
    const = lambda shape: pl.BlockSpec(shape, lambda bi, i: (0,) * len(shape), pipeline_mode=pl.Buffered(1))
    e_rows = e1.shape[1]
    g_rows = SUBLANES if seq_len >= tm else tm
    return pl.pallas_call(
        functools.partial(_ffn_kernel, tm=tm, seq_len=seq_len),
        grid=(nb, n_tiles),
        in_specs=[pl.BlockSpec((1, tm, D_MODEL), lambda bi, i: (bi, i, 0)),
                  const((1, D_MODEL)), const((D_MODEL, D_FF)), const((D_MODEL, D_FF)),
                  const((3, D_FF)), const((1, D_FF)), const((D_FF, D_MODEL)),
                  pl.BlockSpec((1, e_rows, D_FF), lambda bi, i: (bi, 0, 0)),
                  pl.BlockSpec((1, e_rows, D_FF), lambda bi, i: (bi, 0, 0))],
        out_specs=[pl.BlockSpec((1, tm, D_MODEL), lambda bi, i: (bi, i, 0)),
                   pl.BlockSpec((1, g_rows, D_FF), lambda bi, i: (bi * n_tiles + i, 0, 0))],
        out_shape=[jax.ShapeDtypeStruct((nb, s, D_MODEL), F32),
                   jax.ShapeDtypeStruct((nb * n_tiles, g_rows, D_FF), F32)],
        scratch_shapes=[pltpu.VMEM((tm + SUBLANES, FF_CHUNK), F32), pltpu.VMEM((SUBLANES, D_FF), F32)],
        compiler_params=_cparams(("arbitrary", "arbitrary")),
        name="ffn",
    )(x3d, g_ffn, wg, wu, conv_w, conv_b, wd, e1, e2)


def _ple_kernel(x_ref, p_ref, gp_ref, wpg_ref, wpp_ref, gfin_ref, o_ref):
    x = x_ref[...]
    gate = jax.nn.sigmoid(_dot(_rms(x, gp_ref[...]).astype(BF16), wpg_ref[...]))
    x = x + _dot(p_ref[...].astype(BF16), wpp_ref[...]) * gate
    o_ref[...] = _rms(x, gfin_ref[...])


def _ple(x2d, p2d, g_ple, wpg, wpp, g_final, tm):
    t = x2d.shape[0]
    const = lambda shape: pl.BlockSpec(shape, lambda i: (0,) * len(shape))
    return pl.pallas_call(
        _ple_kernel,
        grid=(t // tm,),
        in_specs=[pl.BlockSpec((tm, D_MODEL), lambda i: (i, 0)),
                  pl.BlockSpec((tm, PLE_DIM), lambda i: (i, 0)),
                  const((1, D_MODEL)), const((D_MODEL, D_MODEL)), const((PLE_DIM, D_MODEL)),
                  const((1, D_MODEL))],
        out_specs=pl.BlockSpec((tm, D_MODEL), lambda i: (i, 0)),
        out_shape=jax.ShapeDtypeStruct((t, D_MODEL), F32),
        compiler_params=_cparams(("arbitrary",)),
        name="ple",
    )(x2d, p2d, g_ple, wpg, wpp, g_final)


def kernel(x_prompt, x_sample, cache_k_moba, cache_v_moba, cache_k_diff, cache_v_diff, state_conv, page_table,
           p_prompt, p_sample, g_attn, w_in, lam_params, g_moba_out, g_diff_sub, w_o, g_ffn, w_gate, w_up,
           conv_w, conv_b, w_down, g_ple, w_ple_gate, w_ple_proj, g_final):
    assert w_in.shape[0] == 1, "single-layer trunk"
    bp, sp, _ = x_prompt.shape
    bs, ss, _ = x_sample.shape
    assert sp >= 2 and ss >= 2, "the returned conv state is cut from this step's own gate rows"
    n_pool = cache_k_moba.shape[1]

    row = lambda v: v.reshape(1, -1)
    w_in_b = w_in[0].astype(BF16)
    col = lambda k: w_in_b[:, k * GROUP_WIDTH:(k + 1) * GROUP_WIDTH]
    w_o_b = w_o[0].astype(BF16)
    w_gate_b = w_gate[0].astype(BF16)
    w_up_b = w_up[0].astype(BF16)
    w_down_b = w_down[0].astype(BF16)
    w_pg_b = w_ple_gate[0].astype(BF16)
    w_pp_b = w_ple_proj[0].astype(BF16)
    slopes_a = _alibi_slopes(N_HEADS_MOBA)
    slopes_d = _alibi_slopes(N_HEADS_DIFF)
    lam_p = lam_params[0].astype(F32)

    def tail(oa, od, x3d, p3d, e1, e2, tm, seq_len):
        nb, s, _ = x3d.shape
        t = nb * s
        x1 = _post(oa.reshape(t, GROUP_WIDTH), od.reshape(t, GROUP_WIDTH), x3d.reshape(t, D_MODEL),
                   row(g_moba_out[0]), row(g_diff_sub[0]), w_o_b, min(tm, t))
        x2, g_tail = _ffn(x1.reshape(nb, s, D_MODEL), row(g_ffn[0]), w_gate_b, w_up_b, conv_w[0], row(conv_b[0]),
                          w_down_b, e1, e2, tm, seq_len)
        y = _ple(x2.reshape(t, D_MODEL), p3d.reshape(t, PLE_DIM), row(g_ple[0]), w_pg_b, w_pp_b, row(g_final),
                 min(tm, t))
        return y.reshape(nb, s, D_MODEL), g_tail

    w_rm = jnp.concatenate([col(0), col(3), col(5), col(1), col(4)], axis=1)
    w_t = jnp.concatenate([col(1), col(2), col(4), col(5)], axis=1).T
    qa, qd, vd, ka_b, kd_b, means, kat, vat, kdt, vat_b, vdt_b = _proj_prompt(x_prompt, row(g_attn[0]), w_rm, w_t, 512)
    oa = _moba_prompt(slopes_a, qa, means.reshape(bp, sp // MOBA_BLOCK, GROUP_WIDTH), ka_b, vat_b)
    od = _diff_prompt(slopes_d, lam_p, qd, kd_b, vdt_b)
    tm_p = 512
    dummy = jnp.zeros((bp, SUBLANES, D_FF), F32)
    y_prompt, g_tail_p = tail(oa, od, x_prompt, p_prompt[0], dummy, dummy, tm_p, sp)
    conv_prompt = g_tail_p.reshape(bp, sp // tm_p, SUBLANES, D_FF)[:, -1, SUBLANES - 2:]
    outs_p = (jnp.transpose(kat, (0, 3, 1, 2))[None], jnp.transpose(vat, (0, 3, 1, 2))[None],
              jnp.transpose(kdt.reshape(bp, N_HEADS_DIFF, 2, HEAD_DIM, sp), (0, 4, 1, 2, 3))[None],
              vd.reshape(1, bp, sp, N_HEADS_DIFF, 2 * HEAD_DIM), conv_prompt[None])

    ts = bs * ss
    qa, ka, va, qd, kd, vd = _proj_rows(x_sample.reshape(1, ts, D_MODEL), row(g_attn[0]), w_in_b, ts)
    sh = lambda a: a.reshape(bs, ss, GROUP_WIDTH)
    kt_pages = lambda cch: jnp.moveaxis(cch[0].reshape(n_pool, PAGE_SIZE, N_KT_HEADS, HEAD_DIM), 1, 3)
    v_pages = cache_v_diff[0].reshape(n_pool, PAGE_SIZE * N_HEADS_DIFF, 2 * HEAD_DIM)
    oa = _moba_sample(page_table, slopes_a, sh(qa), sh(ka), sh(va), kt_pages(cache_k_moba), kt_pages(cache_v_moba))
    od = _diff_sample(page_table, slopes_d, lam_p, sh(qd), sh(kd), sh(vd), kt_pages(cache_k_diff), v_pages)
    prev = state_conv[0]
    zero = jnp.zeros((bs, ss - 1, D_FF), F32)
    e1 = jnp.concatenate([prev[:, 1:2], zero], axis=1).reshape(1, ts, D_FF)
    e2 = jnp.concatenate([prev, zero[:, 1:]], axis=1).reshape(1, ts, D_FF)
    y_sample, g_tail_s = tail(oa.reshape(1, ts, GROUP_WIDTH), od.reshape(1, ts, GROUP_WIDTH),
                              x_sample.reshape(1, ts, D_MODEL), p_sample[0].reshape(1, ts, PLE_DIM), e1, e2, ts, ss)
    conv_sample = g_tail_s.reshape(bs, ss, D_FF)[:, ss - 2:]
    outs_s = (ka.reshape(1, bs, ss, N_HEADS_MOBA, HEAD_DIM), va.reshape(1, bs, ss, N_HEADS_MOBA, HEAD_DIM),
              kd.reshape(1, bs, ss, N_HEADS_DIFF, 2, HEAD_DIM), vd.reshape(1, bs, ss, N_HEADS_DIFF, 2 * HEAD_DIM),
              conv_sample[None])
    return (y_prompt, y_sample.reshape(bs, ss, D_MODEL)) + outs_p + outs_s
```

```python
import functools
import math

import jax
import jax.numpy as jnp
import numpy as np
from jax import lax
from jax.experimental import pallas as pl
from jax.experimental.pallas import tpu as pltpu

F32 = jnp.float32
BF16 = jnp.bfloat16

D_MODEL = 1024
HEAD_DIM = 64
N_HEADS_MOBA = 8
N_HEADS_DIFF = 4
N_KT_HEADS = 8
GROUP_WIDTH = 512
MOBA_BLOCK = 256
MOBA_TOPK = 3
PAGE_SIZE = 128
D_FF = 2816
FF_CHUNK = 256
N_FF_CHUNKS = D_FF // FF_CHUNK
PLE_DIM = 256
RMS_EPS = 1e-6
LAM_INIT = 0.8 - 0.6 * math.exp(-0.3 * 0)
QK_SCALE = HEAD_DIM ** -0.5
NEG_INF = float("-inf")
MASKED = -1e30

LANES = 128
SUBLANES = 8
VMEM_LIMIT = 56 * 1024 * 1024
PAGES_PER_STEP = 32
CHUNK_KEYS = PAGES_PER_STEP * PAGE_SIZE
HIGHEST = lax.Precision.HIGHEST

FEAT_QROW = 0
FEAT_QBLK = 1
FEAT_KROW = 2
FEAT_KBLK = 3
FEAT_MASK0 = SUBLANES


def _cparams(sem):
    return pltpu.CompilerParams(dimension_semantics=sem, vmem_limit_bytes=VMEM_LIMIT)


def _rms(x, g):
    return (x * lax.rsqrt(jnp.mean(x * x, axis=-1, keepdims=True) + RMS_EPS)) * g


def _dot_nt(a, b, precision=None):
    return lax.dot_general(a, b, (((1,), (1,)), ((), ())), precision=precision,
                           preferred_element_type=F32)


def _dot(a, b, precision=None):
    return jnp.dot(a, b, precision=precision, preferred_element_type=F32)


def _alibi_slopes(n_heads):
    return jnp.asarray(2.0 ** (-8.0 * np.arange(1, n_heads + 1) / n_heads), dtype=F32)


def _top3_select(gate, valid, axis):
    n = gate.shape[axis]
    idx = lax.broadcasted_iota(jnp.int32, gate.shape, axis)
    g = jnp.where(valid, gate, NEG_INF)
    sel = jnp.zeros(gate.shape, jnp.bool_)
    for _ in range(MOBA_TOPK):
        m = jnp.max(g, axis=axis, keepdims=True)
        first = jnp.min(jnp.where(g == m, idx, n), axis=axis, keepdims=True)
        pick = (idx == first) & (m > NEG_INF)
        sel = sel | (pick & (m < float("inf")))
        g = jnp.where(pick, NEG_INF, g)
    return sel


def _proj_rows_kernel(x_ref, g_ref, w_ref, *out_refs):
    h = _rms(x_ref[0], g_ref[...]).astype(BF16)
    for idx, o_ref in enumerate(out_refs):
        o_ref[0] = _dot(h, w_ref[:, idx * GROUP_WIDTH:(idx + 1) * GROUP_WIDTH])


def _proj_rows(x3d, g, w, tm):
    nb, s, _ = x3d.shape
    n_out = w.shape[1] // GROUP_WIDTH
    const = lambda shape: pl.BlockSpec(shape, lambda bi, i: (0,) * len(shape))
    return pl.pallas_call(
        _proj_rows_kernel,
        grid=(nb, s // tm),
        in_specs=[pl.BlockSpec((1, tm, D_MODEL), lambda bi, i: (bi, i, 0)), const((1, D_MODEL)), const(w.shape)],
        out_specs=[pl.BlockSpec((1, tm, GROUP_WIDTH), lambda bi, i: (bi, i, 0))] * n_out,
        out_shape=[jax.ShapeDtypeStruct((nb, s, GROUP_WIDTH), F32)] * n_out,
        compiler_params=_cparams(("arbitrary", "arbitrary")),
        name="proj_rows",
    )(x3d, g, w)


def _proj_prompt_kernel(x_ref, g_ref, wr_ref, wt_ref, qa_ref, qd_ref, vd_ref, kab_ref, kdb_ref, mean_ref,
                        kat_ref, vat_ref, kdt_ref, vatb_ref, vdtb_ref):
    h = _rms(x_ref[0], g_ref[...]).astype(BF16)
    tm = h.shape[0]
    col = lambda k: wr_ref[:, k * GROUP_WIDTH:(k + 1) * GROUP_WIDTH]
    rows = lambda k: wt_ref[k * GROUP_WIDTH:(k + 1) * GROUP_WIDTH, :]
    qa_ref[0] = _dot(h, col(0))
    qd_ref[0] = _dot(h, col(1))
    vd_ref[0] = _dot(h, col(2))
    ka = _dot(h, col(3))
    kab_ref[0] = ka.astype(BF16)
    for r in range(tm // MOBA_BLOCK):
        mean_ref[0, r] = jnp.mean(ka[r * MOBA_BLOCK:(r + 1) * MOBA_BLOCK], axis=0, keepdims=True)
    kdb_ref[0] = _dot(h, col(4)).astype(BF16)
    kat_ref[0] = _dot_nt(rows(0), h).reshape(N_KT_HEADS, HEAD_DIM, tm)
    vat = _dot_nt(rows(1), h).reshape(N_KT_HEADS, HEAD_DIM, tm)
    vat_ref[0] = vat
    vatb_ref[0] = vat.astype(BF16)
    kdt_ref[0] = _dot_nt(rows(2), h).reshape(N_KT_HEADS, HEAD_DIM, tm)
    vdtb_ref[0] = _dot_nt(rows(3), h).astype(BF16).reshape(N_HEADS_DIFF, 2 * HEAD_DIM, tm)


def _proj_prompt(x3d, g, wr, wt, tm):
    nb, s, _ = x3d.shape
    const = lambda shape: pl.BlockSpec(shape, lambda bi, i: (0,) * len(shape))
    rm_spec = pl.BlockSpec((1, tm, GROUP_WIDTH), lambda bi, i: (bi, i, 0))
    kt_spec = pl.BlockSpec((1, N_KT_HEADS, HEAD_DIM, tm), lambda bi, i: (bi, 0, 0, i))
    rm = lambda dt: jax.ShapeDtypeStruct((nb, s, GROUP_WIDTH), dt)
    kt = lambda dt: jax.ShapeDtypeStruct((nb, N_KT_HEADS, HEAD_DIM, s), dt)
    return pl.pallas_call(
        _proj_prompt_kernel,
        grid=(nb, s // tm),
        in_specs=[pl.BlockSpec((1, tm, D_MODEL), lambda bi, i: (bi, i, 0)), const((1, D_MODEL)),
                  const(wr.shape), const(wt.shape)],
        out_specs=[rm_spec] * 5
        + [pl.BlockSpec((1, tm // MOBA_BLOCK, 1, GROUP_WIDTH), lambda bi, i: (bi, i, 0, 0))]
        + [kt_spec] * 4
        + [pl.BlockSpec((1, N_HEADS_DIFF, 2 * HEAD_DIM, tm), lambda bi, i: (bi, 0, 0, i))],
        out_shape=[rm(F32), rm(F32), rm(F32), rm(BF16), rm(BF16),
                   jax.ShapeDtypeStruct((nb, s // MOBA_BLOCK, 1, GROUP_WIDTH), F32),
                   kt(F32), kt(F32), kt(F32), kt(BF16),
                   jax.ShapeDtypeStruct((nb, N_HEADS_DIFF, 2 * HEAD_DIM, s), BF16)],
        compiler_params=_cparams(("arbitrary", "arbitrary")),
        name="proj_prompt",
    )(x3d, g, wr, wt)


def _build_key_features(k_ref, kaug_ref, slopes, n_blocks):
    lane = lax.broadcasted_iota(jnp.int32, (MOBA_BLOCK, LANES), 1)
    krow = lax.broadcasted_iota(jnp.int32, (MOBA_BLOCK, LANES), 0).astype(F32)
    for mp in range(2):
        f = lane - HEAD_DIM * (1 - mp)
        own_half = (lane >= HEAD_DIM * mp) & (lane < HEAD_DIM * (mp + 1))
        slope = slopes[mp]
        for j in range(n_blocks):
            feats = jnp.where((f == FEAT_QROW) | (f == FEAT_QBLK) | (f == FEAT_MASK0 + j), 1.0,
                              jnp.where(f == FEAT_KROW, slope * krow,
                                        jnp.where(f == FEAT_KBLK, slope * float(MOBA_BLOCK * j), 0.0)))
            rs = slice(j * MOBA_BLOCK, (j + 1) * MOBA_BLOCK)
            kaug_ref[mp, rs, :] = jnp.where(own_half, k_ref[0, rs, :], feats.astype(BF16))


def _query_features(q_ref, slopes, i, masks):
    tq = q_ref.shape[1]
    qt = (q_ref[0] * QK_SCALE).T
    frow = lax.broadcasted_iota(jnp.int32, (SUBLANES, tq), 0)
    qrow = lax.broadcasted_iota(jnp.int32, (SUBLANES, tq), 1).astype(F32)
    out = []
    for mp in range(2):
        slope = slopes[mp]
        head = jnp.where(frow == FEAT_QROW, -slope * qrow,
                         jnp.where(frow == FEAT_QBLK, -slope * (i * MOBA_BLOCK).astype(F32),
                                   jnp.where((frow == FEAT_KROW) | (frow == FEAT_KBLK), 1.0, 0.0)))
        pieces = [head]
        used = SUBLANES
        if masks is not None:
            pieces.append(masks[mp])
            used += masks[mp].shape[0]
        pieces.append(jnp.zeros((HEAD_DIM - used, tq), F32))
        feats = jnp.concatenate(pieces, axis=0)
        own = qt[mp * HEAD_DIM:(mp + 1) * HEAD_DIM]
        out.append(jnp.concatenate([own, feats] if mp == 0 else [feats, own], axis=0).astype(BF16))
    return out


def _softmax_step(state, s_tiles, pv):
    m_i, l_i, acc = state
    m_new = m_i
    for s in s_tiles:
        m_new = jnp.maximum(m_new, jnp.max(s, axis=0, keepdims=True))
    alpha = jnp.exp(m_i - m_new)
    p_tiles = [jnp.exp(s - m_new) for s in s_tiles]
    l_new = alpha * l_i
    for p in p_tiles:
        l_new = l_new + jnp.sum(p, axis=0, keepdims=True)
    return m_new, l_new, alpha * acc + pv([p.astype(BF16) for p in p_tiles])


def _attend_blocks(i, kaug_ref, qaug, pv_fns, widths):
    tq = qaug[0].shape[1]
    kblock = lambda mp, j: kaug_ref[mp, pl.ds(pl.multiple_of(j * MOBA_BLOCK, MOBA_BLOCK), MOBA_BLOCK), :]
    krow = lax.broadcasted_iota(jnp.int32, (MOBA_BLOCK, tq), 0)
    qcol = lax.broadcasted_iota(jnp.int32, (MOBA_BLOCK, tq), 1)
    causal = krow <= qcol
    j_prev = jnp.maximum(i - 1, 0)
    prev_off = jnp.where(i % 2 == 1, 0.0, MASKED).astype(F32)
    n_pairs = i // 2

    def pair_scores(j2):
        return tuple((_dot(kblock(mp, 2 * j2), qaug[mp]), _dot(kblock(mp, 2 * j2 + 1), qaug[mp]))
                     for mp in range(2))

    s_first = [(_dot(kblock(mp, j_prev), qaug[mp]) + prev_off,
                jnp.where(causal, _dot(kblock(mp, i), qaug[mp]), MASKED)) for mp in range(2)]
    s_next = pair_scores(0)
    states = []
    for mp in range(2):
        init = (jnp.full((1, tq), NEG_INF, F32), jnp.zeros((1, tq), F32), jnp.zeros((widths[mp], tq), F32))
        states.append(_softmax_step(init, list(s_first[mp]), functools.partial(pv_fns[mp], j_prev, i)))

    def body(j2, carry):
        states, s_cur = carry
        s_next = pair_scores(jnp.minimum(j2 + 1, jnp.maximum(n_pairs - 1, 0)))
        states = tuple(
            _softmax_step(states[mp], list(s_cur[mp]), functools.partial(pv_fns[mp], 2 * j2, 2 * j2 + 1))
            for mp in range(2))
        return states, s_next

    states, _ = lax.fori_loop(0, n_pairs, body, (tuple(states), s_next))
    return [(st[1], st[2]) for st in states]


def _moba_prompt_kernel(slopes_ref, q_ref, mean_ref, k_ref, vt_ref, o_ref, kaug_ref, vtb_ref, *, n_blocks):
    hp = pl.program_id(1)
    i = pl.program_id(2)
    tq = MOBA_BLOCK
    slopes = [slopes_ref[hp * 2], slopes_ref[hp * 2 + 1]]

    @pl.when(i == 0)
    def _():
        _build_key_features(k_ref, kaug_ref, slopes, n_blocks)
        for hh in range(2):
            for j in range(n_blocks):
                vtb_ref[hh * n_blocks + j] = vt_ref[0, hh, :, j * MOBA_BLOCK:(j + 1) * MOBA_BLOCK]

    lane = lax.broadcasted_iota(jnp.int32, (n_blocks, LANES), 1)
    blk = lax.broadcasted_iota(jnp.int32, (n_blocks, tq), 0)
    means = mean_ref[0]
    masks = []
    for hh in range(2):
        own_half = (lane >= HEAD_DIM * hh) & (lane < HEAD_DIM * (hh + 1))
        gate_t = _dot_nt(jnp.where(own_half, means, 0.0), q_ref[0], precision=HIGHEST)
        sel = _top3_select(gate_t, blk < i, axis=0)
        masks.append(jnp.where(sel | (blk == i), 0.0, MASKED))
    qaug = _query_features(q_ref, slopes, i, masks)

    def pv(hh, ja, jb, p_tiles):
        return _dot(vtb_ref[hh * n_blocks + ja], p_tiles[0]) + _dot(vtb_ref[hh * n_blocks + jb], p_tiles[1])

    res = _attend_blocks(i, kaug_ref, qaug, [functools.partial(pv, 0), functools.partial(pv, 1)],
                         [HEAD_DIM, HEAD_DIM])
    out_t = jnp.concatenate([acc / l_i for l_i, acc in res], axis=0)
    o_ref[0] = out_t.T


def _moba_prompt(slopes, qa, means, ka_b, vat_b):
    b, s, _ = qa.shape
    n_blocks = s // MOBA_BLOCK
    blk_spec = pl.BlockSpec((1, MOBA_BLOCK, LANES), lambda bi, hp, i, sl: (bi, i, hp))
    return pl.pallas_call(
        functools.partial(_moba_prompt_kernel, n_blocks=n_blocks),
        grid_spec=pltpu.PrefetchScalarGridSpec(
            num_scalar_prefetch=1,
            grid=(b, N_HEADS_MOBA // 2, n_blocks),
            in_specs=[blk_spec,
                      pl.BlockSpec((1, n_blocks, LANES), lambda bi, hp, i, sl: (bi, 0, hp)),
                      pl.BlockSpec((1, s, LANES), lambda bi, hp, i, sl: (bi, 0, hp)),
                      pl.BlockSpec((1, 2, HEAD_DIM, s), lambda bi, hp, i, sl: (bi, hp, 0, 0))],
            out_specs=blk_spec,
            scratch_shapes=[pltpu.VMEM((2, s, LANES), BF16),
                            pltpu.VMEM((2 * n_blocks, HEAD_DIM, MOBA_BLOCK), BF16)]),
        out_shape=jax.ShapeDtypeStruct((b, s, GROUP_WIDTH), F32),
        compiler_params=_cparams(("arbitrary", "arbitrary", "arbitrary")),
        name="moba_prompt",
    )(slopes, qa, means, ka_b, vat_b)


def _lambda(lp_ref):
    lp = lp_ref[...]
    a = jnp.sum(lp[0:1] * lp[1:2], axis=1, keepdims=True)
    c = jnp.sum(lp[2:3] * lp[3:4], axis=1, keepdims=True)
    return jnp.exp(a) - jnp.exp(c) + LAM_INIT


def _diff_prompt_step(h, i, slopes_ref, lp_ref, q_ref, k_ref, vt_ref, o_ref, kaug_ref, vtb_ref, n_blocks):
    slopes = [slopes_ref[h], slopes_ref[h]]

    @pl.when(i == 0)
    def _():
        _build_key_features(k_ref, kaug_ref, slopes, n_blocks)
        for j in range(n_blocks):
            vtb_ref[j] = vt_ref[0, 0, :, j * MOBA_BLOCK:(j + 1) * MOBA_BLOCK]

    qaug = _query_features(q_ref, slopes, i, None)

    def pv(ja, jb, p_tiles):
        return _dot(vtb_ref[ja], p_tiles[0]) + _dot(vtb_ref[jb], p_tiles[1])

    (l1, acc1), (l2, acc2) = _attend_blocks(i, kaug_ref, qaug, [pv, pv], [2 * HEAD_DIM, 2 * HEAD_DIM])
    out_t = acc1 / l1 - _lambda(lp_ref) * (acc2 / l2)
    o_ref[0] = out_t.T


def _block_diag_q(q4):
    n_q = q4.shape[0]
    rows = N_KT_HEADS * n_q
    tiled = jnp.concatenate([q4] * N_KT_HEADS, axis=0)
    r = lax.broadcasted_iota(jnp.int32, (rows, GROUP_WIDTH), 0)
    c = lax.broadcasted_iota(jnp.int32, (rows, GROUP_WIDTH), 1)
    return jnp.where(r // n_q == c // HEAD_DIM, tiled, 0.0)


def _pad_rows(x, rows):
    return jnp.concatenate([x, jnp.zeros((rows - x.shape[0], x.shape[1]), x.dtype)], axis=0)


def _row_slopes(slopes_ref, head_of_row, n_heads):
    slope = jnp.zeros(head_of_row.shape, F32)
    for hh in range(n_heads):
        slope = jnp.where(head_of_row == hh, slopes_ref[hh], slope)
    return slope


def _new_key_scores(qb, kn_ref, slope, t, n_q):
    kn = _pad_rows(kn_ref[0], SUBLANES)
    sn = _dot_nt(qb, kn.astype(BF16))
    u = lax.broadcasted_iota(jnp.int32, sn.shape, 1)
    return jnp.where((u <= t) & (u < n_q), sn - slope * (t - u).astype(F32), NEG_INF)


def _moba_sample_kernel(pt_ref, slopes_ref, q_ref, kn_ref, vn_ref, *rest, n_q, n_chunks):
    k_pages = rest[:PAGES_PER_STEP]
    v_pages = rest[PAGES_PER_STEP:2 * PAGES_PER_STEP]
    o_ref = rest[2 * PAGES_PER_STEP]
    s_ref, p_ref, gate_ref, acc_ref, l_ref, pn_ref = rest[2 * PAGES_PER_STEP + 1:]
    c = pl.program_id(1)
    rows = N_HEADS_MOBA * n_q
    n_blocks = n_chunks * CHUNK_KEYS // MOBA_BLOCK
    blocks_per_step = CHUNK_KEYS // MOBA_BLOCK
    past_len = n_chunks * CHUNK_KEYS
    qb = (_block_diag_q(q_ref[0]) * QK_SCALE).astype(BF16)

    @pl.when(c == 0)
    def _():
        gate_ref[...] = jnp.zeros(gate_ref.shape, F32)

    @pl.when(c < n_chunks)
    def _():
        blk_lane = lax.broadcasted_iota(jnp.int32, (rows, n_blocks), 1)
        gate = gate_ref[...]
        for p in range(PAGES_PER_STEP):
            kp = k_pages[p][0].reshape(GROUP_WIDTH, PAGE_SIZE)
            sp = _dot(qb, kp.astype(BF16))
            s_ref[:, pl.ds(pl.multiple_of(c * CHUNK_KEYS + p * PAGE_SIZE, PAGE_SIZE), PAGE_SIZE)] = sp
            if p % 2 == 0:
                pair = sp
            else:
                bsum = jnp.sum(pair + sp, axis=1, keepdims=True) / MOBA_BLOCK
                gate = jnp.where(blk_lane == c * blocks_per_step + p // 2, bsum, gate)
        gate_ref[...] = gate

    @pl.when(c == n_chunks - 1)
    def _():
        gate = gate_ref[...]
        sel = _top3_select(gate, jnp.ones(gate.shape, jnp.bool_), axis=1)
        eb = lax.broadcasted_iota(jnp.int32, (n_blocks, past_len), 0)
        ek = lax.broadcasted_iota(jnp.int32, (n_blocks, past_len), 1)
        expand = jnp.where(ek // MOBA_BLOCK == eb, 1.0, 0.0).astype(BF16)
        selk = _dot(jnp.where(sel, 1.0, 0.0).astype(BF16), expand)
        r = lax.broadcasted_iota(jnp.int32, (rows, 1), 0)
        t = r % n_q
        slope = _row_slopes(slopes_ref, r // n_q, N_HEADS_MOBA)
        kpos = lax.broadcasted_iota(jnp.int32, (rows, past_len), 1)
        s = jnp.where(selk > 0.5, s_ref[...] - slope * (past_len + t - kpos).astype(F32), NEG_INF)
        sn = _new_key_scores(qb, kn_ref, slope, t, n_q)
        m = jnp.maximum(jnp.max(s, axis=1, keepdims=True), jnp.max(sn, axis=1, keepdims=True))
        p = jnp.exp(s - m)
        pn = jnp.exp(sn - m)
        l_ref[...] = jnp.sum(p, axis=1, keepdims=True) + jnp.sum(pn, axis=1, keepdims=True)
        p_ref[...] = p.astype(BF16)
        pn_ref[...] = pn
        acc_ref[...] = jnp.zeros(acc_ref.shape, F32)

    @pl.when(c >= n_chunks)
    def _():
        cv = c - n_chunks
        acc = acc_ref[...]
        for p in range(PAGES_PER_STEP):
            pp = p_ref[:, pl.ds(pl.multiple_of(cv * CHUNK_KEYS + p * PAGE_SIZE, PAGE_SIZE), PAGE_SIZE)]
            vp = v_pages[p][0].reshape(GROUP_WIDTH, PAGE_SIZE)
            acc = acc + _dot_nt(pp, vp.astype(BF16))
        acc_ref[...] = acc

    @pl.when(c == 2 * n_chunks - 1)
    def _():
        vn = _pad_rows(vn_ref[0], SUBLANES)
        acc = acc_ref[...] + _dot(pn_ref[...].astype(BF16), vn.astype(BF16))
        acc = acc / l_ref[...]
        r = lax.broadcasted_iota(jnp.int32, (rows, GROUP_WIDTH), 0)
        cc = lax.broadcasted_iota(jnp.int32, (rows, GROUP_WIDTH), 1)
        acc = jnp.where(r // n_q == cc // HEAD_DIM, acc, 0.0)
        out = acc[0:n_q]
        for hh in range(1, N_HEADS_MOBA):
            out = out + acc[hh * n_q:(hh + 1) * n_q]
        o_ref[0] = out


def _page_specs(block_shape, n_chunks, phase_offset, step_of):
    zeros = (0,) * (len(block_shape) - 1)

    def make(p):
        def index_map(*args):
            pt = args[-2]
            bi, c = step_of(*args[:-2])
            rel = c - phase_offset
            chunk = jnp.where(rel < 0, n_chunks - 1, jnp.minimum(rel, n_chunks - 1))
            seq = jnp.where(rel < 0, jnp.maximum(bi - 1, 0), bi)
            return (pt[seq, chunk * PAGES_PER_STEP + p],) + zeros
        return pl.BlockSpec(block_shape, index_map)
    return [make(p) for p in range(PAGES_PER_STEP)]


KT_PAGE = (1, N_KT_HEADS, HEAD_DIM, PAGE_SIZE)


def _moba_sample(page_table, slopes, qa, ka_new, va_new, cache_kt, cache_vt):
    b, n_q, _ = qa.shape
    n_pages = page_table.shape[1]
    n_chunks = n_pages // PAGES_PER_STEP
    past_len = n_pages * PAGE_SIZE
    rows = N_HEADS_MOBA * n_q
    tok_spec = pl.BlockSpec((1, n_q, GROUP_WIDTH), lambda bi, c, pt, sl: (bi, 0, 0))
    return pl.pallas_call(
        functools.partial(_moba_sample_kernel, n_q=n_q, n_chunks=n_chunks),
        grid_spec=pltpu.PrefetchScalarGridSpec(
            num_scalar_prefetch=2,
            grid=(b, 2 * n_chunks),
            in_specs=[tok_spec, tok_spec, tok_spec] + _page_specs(KT_PAGE, n_chunks, 0, lambda bi, c: (bi, c))
            + _page_specs(KT_PAGE, n_chunks, n_chunks, lambda bi, c: (bi, c)),
            out_specs=tok_spec,
            scratch_shapes=[pltpu.VMEM((rows, past_len), F32), pltpu.VMEM((rows, past_len), BF16),
                            pltpu.VMEM((rows, past_len // MOBA_BLOCK), F32),
                            pltpu.VMEM((rows, GROUP_WIDTH), F32), pltpu.VMEM((rows, 1), F32),
                            pltpu.VMEM((rows, SUBLANES), F32)]),
        out_shape=jax.ShapeDtypeStruct((b, n_q, GROUP_WIDTH), F32),
        compiler_params=_cparams(("arbitrary", "arbitrary")),
        name="moba_sample",
    )(page_table, slopes, qa, ka_new, va_new, *([cache_kt] * PAGES_PER_STEP), *([cache_vt] * PAGES_PER_STEP))


def _diff_sample_step(c, slopes_ref, lp_ref, q_ref, kn_ref, vn_ref, k_pages, v_pages, o_ref,
                      s_ref, m_ref, l_ref, acc_ref, n_q, n_chunks):
    rows = N_KT_HEADS * n_q
    head_rows = 2 * n_q
    e = 2 * HEAD_DIM
    past_len = n_chunks * CHUNK_KEYS
    qb = (_block_diag_q(q_ref[0]) * QK_SCALE).astype(BF16)

    @pl.when(c == 0)
    def _():
        m_ref[...] = jnp.full(m_ref.shape, NEG_INF, F32)
        l_ref[...] = jnp.zeros(l_ref.shape, F32)
        acc_ref[...] = jnp.zeros(acc_ref.shape, F32)

    r = lax.broadcasted_iota(jnp.int32, (rows, 1), 0)
    t = r % n_q
    slope = _row_slopes(slopes_ref, r // head_rows, N_HEADS_DIFF)

    for p in range(PAGES_PER_STEP):
        kp = k_pages[p][0].reshape(GROUP_WIDTH, PAGE_SIZE)
        s_ref[:, p * PAGE_SIZE:(p + 1) * PAGE_SIZE] = _dot(qb, kp.astype(BF16))
    kpos = c * CHUNK_KEYS + lax.broadcasted_iota(jnp.int32, (rows, CHUNK_KEYS), 1)
    s = s_ref[...] - slope * (past_len + t - kpos).astype(F32)
    m_old = m_ref[...]
    m_new = jnp.maximum(m_old, jnp.max(s, axis=1, keepdims=True))
    alpha = jnp.exp(m_old - m_new)
    pexp = jnp.exp(s - m_new)
    l_ref[...] = alpha * l_ref[...] + jnp.sum(pexp, axis=1, keepdims=True)
    m_ref[...] = m_new
    pb = pexp.astype(BF16)
    acc = alpha * acc_ref[...]
    parts = []
    for hh in range(N_HEADS_DIFF):
        a = acc[hh * head_rows:(hh + 1) * head_rows]
        for p in range(PAGES_PER_STEP):
            vh = v_pages[p][0, pl.ds(hh, PAGE_SIZE, stride=N_HEADS_DIFF), :].astype(BF16)
            a = a + _dot(pb[hh * head_rows:(hh + 1) * head_rows, p * PAGE_SIZE:(p + 1) * PAGE_SIZE], vh)
        parts.append(a)
    acc_ref[...] = jnp.concatenate(parts, axis=0)

    @pl.when(c == n_chunks - 1)
    def _():
        sn = _new_key_scores(qb, kn_ref, slope, t, n_q)
        m_o = m_ref[...]
        m_f = jnp.maximum(m_o, jnp.max(sn, axis=1, keepdims=True))
        al = jnp.exp(m_o - m_f)
        pn = jnp.exp(sn - m_f)
        l_f = al * l_ref[...] + jnp.sum(pn, axis=1, keepdims=True)
        pnb = pn.astype(BF16)
        vn = _pad_rows(vn_ref[0], SUBLANES).astype(BF16)
        a_old = al * acc_ref[...]
        lam = _lambda(lp_ref)
        outs = []
        for hh in range(N_HEADS_DIFF):
            rs = slice(hh * head_rows, (hh + 1) * head_rows)
            a = (a_old[rs] + _dot(pnb[rs], vn[:, hh * e:(hh + 1) * e])) / l_f[rs]
            outs.append(a[0:n_q] - lam * a[n_q:2 * n_q])
        o_ref[0] = jnp.concatenate(outs, axis=1)


def _diff_attention_kernel(pt_ref, slopes_ref, lp_ref, q_ref, k_ref, vt_ref, sq_ref, kn_ref, vn_ref, *rest,
                           n_blocks, n_q, n_chunks, n_sample_steps):
    k_pages = rest[:PAGES_PER_STEP]
    v_pages = rest[PAGES_PER_STEP:2 * PAGES_PER_STEP]
    o_ref, so_ref = rest[2 * PAGES_PER_STEP:2 * PAGES_PER_STEP + 2]
    kaug_ref, vtb_ref, s_ref, m_ref, l_ref, acc_ref = rest[2 * PAGES_PER_STEP + 2:]
    h = pl.program_id(1)
    i = pl.program_id(2)
    lin = (pl.program_id(0) * N_HEADS_DIFF + h) * n_blocks + i
    _diff_prompt_step(h, i, slopes_ref, lp_ref, q_ref, k_ref, vt_ref, o_ref, kaug_ref, vtb_ref, n_blocks)

    @pl.when(lin < n_sample_steps)
    def _():
        _diff_sample_step(lin % n_chunks, slopes_ref, lp_ref, sq_ref, kn_ref, vn_ref, k_pages, v_pages, so_ref,
                          s_ref, m_ref, l_ref, acc_ref, n_q, n_chunks)


def _diff_attention(page_table, slopes, lam_params, qd, kd_b, vdt_b, sqd, skd_new, svd_new, cache_kt, cache_v):
    b, s, _ = qd.shape
    n_blocks = s // MOBA_BLOCK
    sb, n_q, _ = sqd.shape
    n_chunks = page_table.shape[1] // PAGES_PER_STEP
    n_sample_steps = sb * n_chunks
    assert b * N_HEADS_DIFF * n_blocks >= n_sample_steps, "sample steps ride on the prompt steps"
    rows = N_KT_HEADS * n_q

    def sample_step(bi, h, i):
        lin = jnp.minimum((bi * N_HEADS_DIFF + h) * n_blocks + i, n_sample_steps - 1)
        return lin // n_chunks, lin % n_chunks

    blk_spec = pl.BlockSpec((1, MOBA_BLOCK, LANES), lambda bi, h, i, pt, sl: (bi, i, h))
    tok_spec = pl.BlockSpec((1, n_q, GROUP_WIDTH), lambda bi, h, i, pt, sl: (sample_step(bi, h, i)[0], 0, 0))
    v_page = (1, PAGE_SIZE * N_HEADS_DIFF, 2 * HEAD_DIM)
    return pl.pallas_call(
        functools.partial(_diff_attention_kernel, n_blocks=n_blocks, n_q=n_q, n_chunks=n_chunks,
                          n_sample_steps=n_sample_steps),
        grid_spec=pltpu.PrefetchScalarGridSpec(
            num_scalar_prefetch=2,
            grid=(b, N_HEADS_DIFF, n_blocks),
            in_specs=[pl.BlockSpec((4, HEAD_DIM), lambda bi, h, i, pt, sl: (0, 0)), blk_spec,
                      pl.BlockSpec((1, s, LANES), lambda bi, h, i, pt, sl: (bi, 0, h)),
                      pl.BlockSpec((1, 1, 2 * HEAD_DIM, s), lambda bi, h, i, pt, sl: (bi, h, 0, 0)),
                      tok_spec, tok_spec, tok_spec]
            + _page_specs(KT_PAGE, n_chunks, 0, sample_step) + _page_specs(v_page, n_chunks, 0, sample_step),
            out_specs=[blk_spec, tok_spec],
            scratch_shapes=[pltpu.VMEM((2, s, LANES), BF16),
                            pltpu.VMEM((n_blocks, 2 * HEAD_DIM, MOBA_BLOCK), BF16),
                            pltpu.VMEM((rows, CHUNK_KEYS), F32), pltpu.VMEM((rows, 1), F32),
                            pltpu.VMEM((rows, 1), F32), pltpu.VMEM((rows, 2 * HEAD_DIM), F32)]),
        out_shape=[jax.ShapeDtypeStruct((b, s, GROUP_WIDTH), F32),
                   jax.ShapeDtypeStruct((sb, n_q, GROUP_WIDTH), F32)],
        compiler_params=_cparams(("arbitrary", "arbitrary", "arbitrary")),
        name="diff_attention",
    )(page_table, slopes, lam_params, qd, kd_b, vdt_b, sqd, skd_new, svd_new,
      *([cache_kt] * PAGES_PER_STEP), *([cache_v] * PAGES_PER_STEP))


def _post_kernel(oa_ref, od_ref, x_ref, gm_ref, gd_ref, wo_ref, o_ref):
    oa = _rms(oa_ref[...], gm_ref[...]).astype(BF16)
    y = x_ref[...] + _dot(oa, wo_ref[0:GROUP_WIDTH, :])
    od = od_ref[...]
    for h in range(N_HEADS_DIFF):
        w = 2 * HEAD_DIM
        odh = (_rms(od[:, h * w:(h + 1) * w], gd_ref[...]) * (1.0 - LAM_INIT)).astype(BF16)
        y = y + _dot(odh, wo_ref[GROUP_WIDTH + h * w:GROUP_WIDTH + (h + 1) * w, :])
    o_ref[...] = y


def _post(oa, od, x2d, g_moba, g_diff, wo_bf16, tm):
    t = x2d.shape[0]
    return pl.pallas_call(
        _post_kernel,
        grid=(t // tm,),
        in_specs=[pl.BlockSpec((tm, GROUP_WIDTH), lambda i: (i, 0)),
                  pl.BlockSpec((tm, GROUP_WIDTH), lambda i: (i, 0)),
                  pl.BlockSpec((tm, D_MODEL), lambda i: (i, 0)),
                  pl.BlockSpec((1, GROUP_WIDTH), lambda i: (0, 0)),
                  pl.BlockSpec((1, 2 * HEAD_DIM), lambda i: (0, 0)),
                  pl.BlockSpec((D_MODEL, D_MODEL), lambda i: (0, 0))],
        out_specs=pl.BlockSpec((tm, D_MODEL), lambda i: (i, 0)),
        out_shape=jax.ShapeDtypeStruct((t, D_MODEL), F32),
        compiler_params=_cparams(("arbitrary",)),
        name="post",
    )(oa, od, x2d, g_moba, g_diff, wo_bf16)


def _ffn_kernel(x_ref, gf_ref, wg_ref, wu_ref, cw_ref, cb_ref, wd_ref, e1_ref, e2_ref,
                o_ref, g_out_ref, gbuf_ref, carry_ref, *, tm, seq_len):
    long_seq = seq_len >= tm
    x = x_ref[0]
    h = _rms(x, gf_ref[...]).astype(BF16)
    if long_seq:
        @pl.when(pl.program_id(1) == 0)
        def _():
            carry_ref[...] = jnp.zeros(carry_ref.shape, F32)
    else:
        tmod = lax.broadcasted_iota(jnp.int32, (tm, 1), 0) % seq_len
    g_rows = g_out_ref.shape[1]
    y = x
    for fc in range(N_FF_CHUNKS):
        cs = slice(fc * FF_CHUNK, (fc + 1) * FF_CHUNK)
        g = _dot(h, wg_ref[:, cs])
        u = _dot(h, wu_ref[:, cs])
        gbuf_ref[SUBLANES:SUBLANES + tm, :] = g
        if long_seq:
            gbuf_ref[0:SUBLANES, :] = carry_ref[:, cs]
            carry_ref[:, cs] = g[tm - SUBLANES:tm]
        else:
            gbuf_ref[0:SUBLANES, :] = jnp.zeros((SUBLANES, FF_CHUNK), F32)
        gm1 = gbuf_ref[SUBLANES - 1:SUBLANES - 1 + tm, :]
        gm2 = gbuf_ref[SUBLANES - 2:SUBLANES - 2 + tm, :]
        if not long_seq:
            gm1 = jnp.where(tmod >= 1, gm1, e1_ref[0, :, cs])
            gm2 = jnp.where(tmod >= 2, gm2, e2_ref[0, :, cs])
        cw = cw_ref[:, cs]
        gc = cw[0:1] * gm2 + cw[1:2] * gm1 + cw[2:3] * g + cb_ref[:, cs]
        act = (jax.nn.gelu(gc) * u).astype(BF16)
        y = y + _dot(act, wd_ref[cs, :])
        g_out_ref[0, :, cs] = g[tm - g_rows:tm]
    o_ref[0] = y


def _ffn(x3d, g_ffn, wg, wu, conv_w, conv_b, wd, e1, e2, tm, seq_len):
    nb, s, _ = x3d.shape
    n_tiles = s // tm
    const = lambda shape: pl.BlockSpec(shape, lambda bi, i: (0,) * len(shape), pipeline_mode=pl.Buffered(1))
    e_rows = e1.shape[1]
    g_rows = SUBLANES if seq_len >= tm else tm
    return pl.pallas_call(
        functools.partial(_ffn_kernel, tm=tm, seq_len=seq_len),
        grid=(nb, n_tiles),
        in_specs=[pl.BlockSpec((1, tm, D_MODEL), lambda bi, i: (bi, i, 0)),
                  const((1, D_MODEL)), const((D_MODEL, D_FF)), const((D_MODEL, D_FF)),
                  const((3, D_FF)), const((1, D_FF)), const((D_FF, D_MODEL)),
                  pl.BlockSpec((1, e_rows, D_FF), lambda bi, i: (bi, 0, 0)),
                  pl.BlockSpec((1, e_rows, D_FF), lambda bi, i: (bi, 0, 0))],
        out_specs=[pl.BlockSpec((1, tm, D_MODEL), lambda bi, i: (bi, i, 0)),
                   pl.BlockSpec((1, g_rows, D_FF), lambda bi, i: (bi * n_tiles + i, 0, 0))],
        out_shape=[jax.ShapeDtypeStruct((nb, s, D_MODEL), F32),
                   jax.ShapeDtypeStruct((nb * n_tiles, g_rows, D_FF), F32)],
        scratch_shapes=[pltpu.VMEM((tm + SUBLANES, FF_CHUNK), F32), pltpu.VMEM((SUBLANES, D_FF), F32)],
        compiler_params=_cparams(("arbitrary", "arbitrary")),
        name="ffn",
    )(x3d, g_ffn, wg, wu, conv_w, conv_b, wd, e1, e2)


def _ple_kernel(x_ref, p_ref, gp_ref, wpg_ref, wpp_ref, gfin_ref, o_ref):
    x = x_ref[...]
    gate = jax.nn.sigmoid(_dot(_rms(x, gp_ref[...]).astype(BF16), wpg_ref[...]))
    x = x + _dot(p_ref[...].astype(BF16), wpp_ref[...]) * gate
    o_ref[...] = _rms(x, gfin_ref[...])


def _ple(x2d, p2d, g_ple, wpg, wpp, g_final, tm):
    t = x2d.shape[0]
    const = lambda shape: pl.BlockSpec(shape, lambda i: (0,) * len(shape))
    return pl.pallas_call(
        _ple_kernel,
        grid=(t // tm,),
        in_specs=[pl.BlockSpec((tm, D_MODEL), lambda i: (i, 0)),
                  pl.BlockSpec((tm, PLE_DIM), lambda i: (i, 0)),
                  const((1, D_MODEL)), const((D_MODEL, D_MODEL)), const((PLE_DIM, D_MODEL)),
                  const((1, D_MODEL))],
        out_specs=pl.BlockSpec((tm, D_MODEL), lambda i: (i, 0)),
        out_shape=jax.ShapeDtypeStruct((t, D_MODEL), F32),
        compiler_params=_cparams(("arbitrary",)),
        name="ple",
    )(x2d, p2d, g_ple, wpg, wpp, g_final)


def kernel(x_prompt, x_sample, cache_k_moba, cache_v_moba, cache_k_diff, cache_v_diff, state_conv, page_table,
           p_prompt, p_sample, g_attn, w_in, lam_params, g_moba_out, g_diff_sub, w_o, g_ffn, w_gate, w_up,
           conv_w, conv_b, w_down, g_ple, w_ple_gate, w_ple_proj, g_final):
    assert w_in.shape[0] == 1, "single-layer trunk"
    bp, sp, _ = x_prompt.shape
    bs, ss, _ = x_sample.shape
    assert sp >= 2 and ss >= 2, "the returned conv state is cut from this step's own gate rows"
    n_pool = cache_k_moba.shape[1]

    row = lambda v: v.reshape(1, -1)
    w_in_b = w_in[0].astype(BF16)
    col = lambda k: w_in_b[:, k * GROUP_WIDTH:(k + 1) * GROUP_WIDTH]
    w_o_b = w_o[0].astype(BF16)
    w_gate_b = w_gate[0].astype(BF16)
    w_up_b = w_up[0].astype(BF16)
    w_down_b = w_down[0].astype(BF16)
    w_pg_b = w_ple_gate[0].astype(BF16)
    w_pp_b = w_ple_proj[0].astype(BF16)
    slopes_a = _alibi_slopes(N_HEADS_MOBA)
    slopes_d = _alibi_slopes(N_HEADS_DIFF)
    lam_p = lam_params[0].astype(F32)

    def tail(oa, od, x3d, p3d, e1, e2, tm, seq_len):
        nb, s, _ = x3d.shape
        t = nb * s
        x1 = _post(oa.reshape(t, GROUP_WIDTH), od.reshape(t, GROUP_WIDTH), x3d.reshape(t, D_MODEL),
                   row(g_moba_out[0]), row(g_diff_sub[0]), w_o_b, min(tm, t))
        x2, g_tail = _ffn(x1.reshape(nb, s, D_MODEL), row(g_ffn[0]), w_gate_b, w_up_b, conv_w[0], row(conv_b[0]),
                          w_down_b, e1, e2, tm, seq_len)
        y = _ple(x2.reshape(t, D_MODEL), p3d.reshape(t, PLE_DIM), row(g_ple[0]), w_pg_b, w_pp_b, row(g_final),
                 min(tm, t))
        return y.reshape(nb, s, D_MODEL), g_tail

    w_rm = jnp.concatenate([col(0), col(3), col(5), col(1), col(4)], axis=1)
    w_t = jnp.concatenate([col(1), col(2), col(4), col(5)], axis=1).T
    qa, qd, vd, ka_b, kd_b, means, kat, vat, kdt, vat_b, vdt_b = _proj_prompt(x_prompt, row(g_attn[0]), w_rm, w_t, 512)
    ts = bs * ss
    sqa, ska, sva, sqd, skd, svd = _proj_rows(x_sample.reshape(1, ts, D_MODEL), row(g_attn[0]), w_in_b, ts)
    sh = lambda a: a.reshape(bs, ss, GROUP_WIDTH)
    kt_pages = lambda cch: jnp.moveaxis(cch[0].reshape(n_pool, PAGE_SIZE, N_KT_HEADS, HEAD_DIM), 1, 3)
    v_pages = cache_v_diff[0].reshape(n_pool, PAGE_SIZE * N_HEADS_DIFF, 2 * HEAD_DIM)

    oa = _moba_prompt(slopes_a, qa, means.reshape(bp, sp // MOBA_BLOCK, GROUP_WIDTH), ka_b, vat_b)
    soa = _moba_sample(page_table, slopes_a, sh(sqa), sh(ska), sh(sva), kt_pages(cache_k_moba),
                       kt_pages(cache_v_moba))
    od, sod = _diff_attention(page_table, slopes_d, lam_p, qd, kd_b, vdt_b, sh(sqd), sh(skd), sh(svd),
                              kt_pages(cache_k_diff), v_pages)

    tm_p = 512
    dummy = jnp.zeros((bp, SUBLANES, D_FF), F32)
    y_prompt, g_tail_p = tail(oa, od, x_prompt, p_prompt[0], dummy, dummy, tm_p, sp)
    conv_prompt = g_tail_p.reshape(bp, sp // tm_p, SUBLANES, D_FF)[:, -1, SUBLANES - 2:]
    outs_p = (jnp.transpose(kat, (0, 3, 1, 2))[None], jnp.transpose(vat, (0, 3, 1, 2))[None],
              jnp.transpose(kdt.reshape(bp, N_HEADS_DIFF, 2, HEAD_DIM, sp), (0, 4, 1, 2, 3))[None],
              vd.reshape(1, bp, sp, N_HEADS_DIFF, 2 * HEAD_DIM), conv_prompt[None])

    oa, od, ka, va, kd, vd = soa, sod, ska, sva, skd, svd
    prev = state_conv[0]
    zero = jnp.zeros((bs, ss - 1, D_FF), F32)
    e1 = jnp.concatenate([prev[:, 1:2], zero], axis=1).reshape(1, ts, D_FF)
    e2 = jnp.concatenate([prev, zero[:, 1:]], axis=1).reshape(1, ts, D_FF)
    y_sample, g_tail_s = tail(oa.reshape(1, ts, GROUP_WIDTH), od.reshape(1, ts, GROUP_WIDTH),
                              x_sample.reshape(1, ts, D_MODEL), p_sample[0].reshape(1, ts, PLE_DIM), e1, e2, ts, ss)
    conv_sample = g_tail_s.reshape(bs, ss, D_FF)[:, ss - 2:]
    outs_s = (ka.reshape(1, bs, ss, N_HEADS_MOBA, HEAD_DIM), va.reshape(1, bs, ss, N_HEADS_MOBA, HEAD_DIM),
              kd.reshape(1, bs, ss, N_HEADS_DIFF, 2, HEAD_DIM), vd.reshape(1, bs, ss, N_HEADS_DIFF, 2 * HEAD_DIM),
              conv_sample[None])
    return (y_prompt, y_sample.reshape(bs, ss, D_MODEL)) + outs_p + outs_s
```

```python
import functools
import math

import jax
import jax.numpy as jnp
import numpy as np
from jax import lax
from jax.experimental import pallas as pl
from jax.experimental.pallas import tpu as pltpu

F32 = jnp.float32
BF16 = jnp.bfloat16

D_MODEL = 1024
HEAD_DIM = 64
N_HEADS_MOBA = 8
N_HEADS_DIFF = 4
N_KT_HEADS = 8
GROUP_WIDTH = 512
MOBA_BLOCK = 256
MOBA_TOPK = 3
PAGE_SIZE = 128
D_FF = 2816
FF_CHUNK = 256
N_FF_CHUNKS = D_FF // FF_CHUNK
PLE_DIM = 256
RMS_EPS = 1e-6
LAM_INIT = 0.8 - 0.6 * math.exp(-0.3 * 0)
QK_SCALE = HEAD_DIM ** -0.5
NEG_INF = float("-inf")
MASKED = -1e30

LANES = 128
SUBLANES = 8
VMEM_LIMIT = 56 * 1024 * 1024
PAGES_PER_STEP = 32
CHUNK_KEYS = PAGES_PER_STEP * PAGE_SIZE
HIGHEST = lax.Precision.HIGHEST

FEAT_QROW = 0
FEAT_QBLK = 1
FEAT_KROW = 2
FEAT_KBLK = 3
FEAT_MASK0 = SUBLANES


def _cparams(sem):
    return pltpu.CompilerParams(dimension_semantics=sem, vmem_limit_bytes=VMEM_LIMIT)


def _rms(x, g):
    return (x * lax.rsqrt(jnp.mean(x * x, axis=-1, keepdims=True) + RMS_EPS)) * g


def _dot_nt(a, b, precision=None):
    return lax.dot_general(a, b, (((1,), (1,)), ((), ())), precision=precision,
                           preferred_element_type=F32)


def _dot(a, b, precision=None):
    return jnp.dot(a, b, precision=precision, preferred_element_type=F32)


def _alibi_slopes(n_heads):
    return jnp.asarray(2.0 ** (-8.0 * np.arange(1, n_heads + 1) / n_heads), dtype=F32)


def _top3_select(gate, valid, axis):
    n = gate.shape[axis]
    idx = lax.broadcasted_iota(jnp.int32, gate.shape, axis)
    g = jnp.where(valid, gate, NEG_INF)
    sel = jnp.zeros(gate.shape, jnp.bool_)
    for _ in range(MOBA_TOPK):
        m = jnp.max(g, axis=axis, keepdims=True)
        first = jnp.min(jnp.where(g == m, idx, n), axis=axis, keepdims=True)
        pick = (idx == first) & (m > NEG_INF)
        sel = sel | (pick & (m < float("inf")))
        g = jnp.where(pick, NEG_INF, g)
    return sel


def _proj_rows_kernel(x_ref, g_ref, w_ref, *out_refs):
    h = _rms(x_ref[0], g_ref[...]).astype(BF16)
    for idx, o_ref in enumerate(out_refs):
        o_ref[0] = _dot(h, w_ref[:, idx * GROUP_WIDTH:(idx + 1) * GROUP_WIDTH])


def _proj_rows(x3d, g, w, tm):
    nb, s, _ = x3d.shape
    n_out = w.shape[1] // GROUP_WIDTH
    const = lambda shape: pl.BlockSpec(shape, lambda bi, i: (0,) * len(shape))
    return pl.pallas_call(
        _proj_rows_kernel,
        grid=(nb, s // tm),
        in_specs=[pl.BlockSpec((1, tm, D_MODEL), lambda bi, i: (bi, i, 0)), const((1, D_MODEL)), const(w.shape)],
        out_specs=[pl.BlockSpec((1, tm, GROUP_WIDTH), lambda bi, i: (bi, i, 0))] * n_out,
        out_shape=[jax.ShapeDtypeStruct((nb, s, GROUP_WIDTH), F32)] * n_out,
        compiler_params=_cparams(("arbitrary", "arbitrary")),
        name="proj_rows",
    )(x3d, g, w)


def _proj_prompt_kernel(x_ref, g_ref, wr_ref, wt_ref, qa_ref, qd_ref, vd_ref, kab_ref, kdb_ref, mean_ref,
                        kat_ref, vat_ref, kdt_ref, vatb_ref, vdtb_ref):
    h = _rms(x_ref[0], g_ref[...]).astype(BF16)
    tm = h.shape[0]
    col = lambda k: wr_ref[:, k * GROUP_WIDTH:(k + 1) * GROUP_WIDTH]
    rows = lambda k: wt_ref[k * GROUP_WIDTH:(k + 1) * GROUP_WIDTH, :]
    qa_ref[0] = _dot(h, col(0))
    qd_ref[0] = _dot(h, col(1))
    vd_ref[0] = _dot(h, col(2))
    ka = _dot(h, col(3))
    kab_ref[0] = ka.astype(BF16)
    for r in range(tm // MOBA_BLOCK):
        mean_ref[0, r] = jnp.mean(ka[r * MOBA_BLOCK:(r + 1) * MOBA_BLOCK], axis=0, keepdims=True)
    kdb_ref[0] = _dot(h, col(4)).astype(BF16)
    kat_ref[0] = _dot_nt(rows(0), h).reshape(N_KT_HEADS, HEAD_DIM, tm)
    vat = _dot_nt(rows(1), h).reshape(N_KT_HEADS, HEAD_DIM, tm)
    vat_ref[0] = vat
    vatb_ref[0] = vat.astype(BF16)
    kdt_ref[0] = _dot_nt(rows(2), h).reshape(N_KT_HEADS, HEAD_DIM, tm)
    vdtb_ref[0] = _dot_nt(rows(3), h).astype(BF16).reshape(N_HEADS_DIFF, 2 * HEAD_DIM, tm)


def _proj_prompt(x3d, g, wr, wt, tm):
    nb, s, _ = x3d.shape
    const = lambda shape: pl.BlockSpec(shape, lambda bi, i: (0,) * len(shape))
    rm_spec = pl.BlockSpec((1, tm, GROUP_WIDTH), lambda bi, i: (bi, i, 0))
    kt_spec = pl.BlockSpec((1, N_KT_HEADS, HEAD_DIM, tm), lambda bi, i: (bi, 0, 0, i))
    rm = lambda dt: jax.ShapeDtypeStruct((nb, s, GROUP_WIDTH), dt)
    kt = lambda dt: jax.ShapeDtypeStruct((nb, N_KT_HEADS, HEAD_DIM, s), dt)
    return pl.pallas_call(
        _proj_prompt_kernel,
        grid=(nb, s // tm),
        in_specs=[pl.BlockSpec((1, tm, D_MODEL), lambda bi, i: (bi, i, 0)), const((1, D_MODEL)),
                  const(wr.shape), const(wt.shape)],
        out_specs=[rm_spec] * 5
        + [pl.BlockSpec((1, tm // MOBA_BLOCK, 1, GROUP_WIDTH), lambda bi, i: (bi, i, 0, 0))]
        + [kt_spec] * 4
        + [pl.BlockSpec((1, N_HEADS_DIFF, 2 * HEAD_DIM, tm), lambda bi, i: (bi, 0, 0, i))],
        out_shape=[rm(F32), rm(F32), rm(F32), rm(BF16), rm(BF16),
                   jax.ShapeDtypeStruct((nb, s // MOBA_BLOCK, 1, GROUP_WIDTH), F32),
                   kt(F32), kt(F32), kt(F32), kt(BF16),
                   jax.ShapeDtypeStruct((nb, N_HEADS_DIFF, 2 * HEAD_DIM, s), BF16)],
        compiler_params=_cparams(("arbitrary", "arbitrary")),
        name="proj_prompt",
    )(x3d, g, wr, wt)


def _build_key_features(k_ref, kaug_ref, slopes, n_blocks):
    lane = lax.broadcasted_iota(jnp.int32, (MOBA_BLOCK, LANES), 1)
    krow = lax.broadcasted_iota(jnp.int32, (MOBA_BLOCK, LANES), 0).astype(F32)
    for mp in range(2):
        f = lane - HEAD_DIM * (1 - mp)
        own_half = (lane >= HEAD_DIM * mp) & (lane < HEAD_DIM * (mp + 1))
        slope = slopes[mp]
        for j in range(n_blocks):
            feats = jnp.where((f == FEAT_QROW) | (f == FEAT_QBLK) | (f == FEAT_MASK0 + j), 1.0,
                              jnp.where(f == FEAT_KROW, slope * krow,
                                        jnp.where(f == FEAT_KBLK, slope * float(MOBA_BLOCK * j), 0.0)))
            rs = slice(j * MOBA_BLOCK, (j + 1) * MOBA_BLOCK)
            kaug_ref[mp, rs, :] = jnp.where(own_half, k_ref[0, rs, :], feats.astype(BF16))


def _query_features(q_ref, slopes, i, masks):
    tq = q_ref.shape[1]
    qt = (q_ref[0] * QK_SCALE).T
    frow = lax.broadcasted_iota(jnp.int32, (SUBLANES, tq), 0)
    qrow = lax.broadcasted_iota(jnp.int32, (SUBLANES, tq), 1).astype(F32)
    out = []
    for mp in range(2):
        slope = slopes[mp]
        head = jnp.where(frow == FEAT_QROW, -slope * qrow,
                         jnp.where(frow == FEAT_QBLK, -slope * (i * MOBA_BLOCK).astype(F32),
                                   jnp.where((frow == FEAT_KROW) | (frow == FEAT_KBLK), 1.0, 0.0)))
        pieces = [head]
        used = SUBLANES
        if masks is not None:
            pieces.append(masks[mp])
            used += masks[mp].shape[0]
        pieces.append(jnp.zeros((HEAD_DIM - used, tq), F32))
        feats = jnp.concatenate(pieces, axis=0)
        own = qt[mp * HEAD_DIM:(mp + 1) * HEAD_DIM]
        out.append(jnp.concatenate([own, feats] if mp == 0 else [feats, own], axis=0).astype(BF16))
    return out


def _softmax_step(state, s_tiles, pv):
    m_i, l_i, acc = state
    m_new = m_i
    for s in s_tiles:
        m_new = jnp.maximum(m_new, jnp.max(s, axis=0, keepdims=True))
    alpha = jnp.exp(m_i - m_new)
    p_tiles = [jnp.exp(s - m_new) for s in s_tiles]
    l_new = alpha * l_i
    for p in p_tiles:
        l_new = l_new + jnp.sum(p, axis=0, keepdims=True)
    return m_new, l_new, alpha * acc + pv([p.astype(BF16) for p in p_tiles])


def _attend_blocks(i, kaug_ref, qaug, pv_fns, widths):
    tq = qaug[0].shape[1]
    kblock = lambda mp, j: kaug_ref[mp, pl.ds(pl.multiple_of(j * MOBA_BLOCK, MOBA_BLOCK), MOBA_BLOCK), :]
    krow = lax.broadcasted_iota(jnp.int32, (MOBA_BLOCK, tq), 0)
    qcol = lax.broadcasted_iota(jnp.int32, (MOBA_BLOCK, tq), 1)
    causal = krow <= qcol
    j_prev = jnp.maximum(i - 1, 0)
    prev_off = jnp.where(i % 2 == 1, 0.0, MASKED).astype(F32)
    n_pairs = i // 2

    def pair_scores(j2):
        return tuple((_dot(kblock(mp, 2 * j2), qaug[mp]), _dot(kblock(mp, 2 * j2 + 1), qaug[mp]))
                     for mp in range(2))

    s_first = [(_dot(kblock(mp, j_prev), qaug[mp]) + prev_off,
                jnp.where(causal, _dot(kblock(mp, i), qaug[mp]), MASKED)) for mp in range(2)]
    s_next = pair_scores(0)
    states = []
    for mp in range(2):
        init = (jnp.full((1, tq), NEG_INF, F32), jnp.zeros((1, tq), F32), jnp.zeros((widths[mp], tq), F32))
        states.append(_softmax_step(init, list(s_first[mp]), functools.partial(pv_fns[mp], j_prev, i)))

    def body(j2, carry):
        states, s_cur = carry
        s_next = pair_scores(jnp.minimum(j2 + 1, jnp.maximum(n_pairs - 1, 0)))
        states = tuple(
            _softmax_step(states[mp], list(s_cur[mp]), functools.partial(pv_fns[mp], 2 * j2, 2 * j2 + 1))
            for mp in range(2))
        return states, s_next

    states, _ = lax.fori_loop(0, n_pairs, body, (tuple(states), s_next))
    return [(st[1], st[2]) for st in states]


def _moba_prompt_step(hp, i, slopes_ref, q_ref, mean_ref, k_ref, vt_ref, o_ref, kaug_ref, vtb_ref, n_blocks):
    tq = MOBA_BLOCK
    slopes = [slopes_ref[hp * 2], slopes_ref[hp * 2 + 1]]

    @pl.when(i == 0)
    def _():
        _build_key_features(k_ref, kaug_ref, slopes, n_blocks)
        for hh in range(2):
            for j in range(n_blocks):
                vtb_ref[hh * n_blocks + j] = vt_ref[0, hh, :, j * MOBA_BLOCK:(j + 1) * MOBA_BLOCK]

    lane = lax.broadcasted_iota(jnp.int32, (n_blocks, LANES), 1)
    blk = lax.broadcasted_iota(jnp.int32, (n_blocks, tq), 0)
    means = mean_ref[0]
    masks = []
    for hh in range(2):
        own_half = (lane >= HEAD_DIM * hh) & (lane < HEAD_DIM * (hh + 1))
        gate_t = _dot_nt(jnp.where(own_half, means, 0.0), q_ref[0], precision=HIGHEST)
        sel = _top3_select(gate_t, blk < i, axis=0)
        masks.append(jnp.where(sel | (blk == i), 0.0, MASKED))
    qaug = _query_features(q_ref, slopes, i, masks)

    def pv(hh, ja, jb, p_tiles):
        return _dot(vtb_ref[hh * n_blocks + ja], p_tiles[0]) + _dot(vtb_ref[hh * n_blocks + jb], p_tiles[1])

    res = _attend_blocks(i, kaug_ref, qaug, [functools.partial(pv, 0), functools.partial(pv, 1)],
                         [HEAD_DIM, HEAD_DIM])
    out_t = jnp.concatenate([acc / l_i for l_i, acc in res], axis=0)
    o_ref[0] = out_t.T


def _lambda(lp_ref):
    lp = lp_ref[...]
    a = jnp.sum(lp[0:1] * lp[1:2], axis=1, keepdims=True)
    c = jnp.sum(lp[2:3] * lp[3:4], axis=1, keepdims=True)
    return jnp.exp(a) - jnp.exp(c) + LAM_INIT


def _diff_prompt_step(h, i, slopes_ref, lp_ref, q_ref, k_ref, vt_ref, o_ref, kaug_ref, vtb_ref, n_blocks):
    slopes = [slopes_ref[h], slopes_ref[h]]

    @pl.when(i == 0)
    def _():
        _build_key_features(k_ref, kaug_ref, slopes, n_blocks)
        for j in range(n_blocks):
            vtb_ref[j] = vt_ref[0, 0, :, j * MOBA_BLOCK:(j + 1) * MOBA_BLOCK]

    qaug = _query_features(q_ref, slopes, i, None)

    def pv(ja, jb, p_tiles):
        return _dot(vtb_ref[ja], p_tiles[0]) + _dot(vtb_ref[jb], p_tiles[1])

    (l1, acc1), (l2, acc2) = _attend_blocks(i, kaug_ref, qaug, [pv, pv], [2 * HEAD_DIM, 2 * HEAD_DIM])
    out_t = acc1 / l1 - _lambda(lp_ref) * (acc2 / l2)
    o_ref[0] = out_t.T


def _block_diag_q(q4):
    n_q = q4.shape[0]
    rows = N_KT_HEADS * n_q
    tiled = jnp.concatenate([q4] * N_KT_HEADS, axis=0)
    r = lax.broadcasted_iota(jnp.int32, (rows, GROUP_WIDTH), 0)
    c = lax.broadcasted_iota(jnp.int32, (rows, GROUP_WIDTH), 1)
    return jnp.where(r // n_q == c // HEAD_DIM, tiled, 0.0)


def _pad_rows(x, rows):
    return jnp.concatenate([x, jnp.zeros((rows - x.shape[0], x.shape[1]), x.dtype)], axis=0)


def _row_slopes(slopes_ref, head_of_row, n_heads):
    slope = jnp.zeros(head_of_row.shape, F32)
    for hh in range(n_heads):
        slope = jnp.where(head_of_row == hh, slopes_ref[hh], slope)
    return slope


def _new_key_scores(qb, kn_ref, slope, t, n_q):
    kn = _pad_rows(kn_ref[0], SUBLANES)
    sn = _dot_nt(qb, kn.astype(BF16))
    u = lax.broadcasted_iota(jnp.int32, sn.shape, 1)
    return jnp.where((u <= t) & (u < n_q), sn - slope * (t - u).astype(F32), NEG_INF)


def _moba_sample_step(c, slopes_ref, q_ref, kn_ref, vn_ref, k_pages, v_pages, o_ref,
                      s_ref, p_ref, gate_ref, acc_ref, l_ref, pn_ref, n_q, n_chunks):
    rows = N_HEADS_MOBA * n_q
    n_blocks = n_chunks * CHUNK_KEYS // MOBA_BLOCK
    blocks_per_step = CHUNK_KEYS // MOBA_BLOCK
    past_len = n_chunks * CHUNK_KEYS
    qb = (_block_diag_q(q_ref[0]) * QK_SCALE).astype(BF16)

    @pl.when(c == 0)
    def _():
        gate_ref[...] = jnp.zeros(gate_ref.shape, F32)

    @pl.when(c < n_chunks)
    def _():
        blk_lane = lax.broadcasted_iota(jnp.int32, (rows, n_blocks), 1)
        gate = gate_ref[...]
        for p in range(PAGES_PER_STEP):
            kp = k_pages[p][0].reshape(GROUP_WIDTH, PAGE_SIZE)
            sp = _dot(qb, kp.astype(BF16))
            s_ref[:, pl.ds(pl.multiple_of(c * CHUNK_KEYS + p * PAGE_SIZE, PAGE_SIZE), PAGE_SIZE)] = sp
            if p % 2 == 0:
                pair = sp
            else:
                bsum = jnp.sum(pair + sp, axis=1, keepdims=True) / MOBA_BLOCK
                gate = jnp.where(blk_lane == c * blocks_per_step + p // 2, bsum, gate)
        gate_ref[...] = gate

    @pl.when(c == n_chunks - 1)
    def _():
        gate = gate_ref[...]
        sel = _top3_select(gate, jnp.ones(gate.shape, jnp.bool_), axis=1)
        eb = lax.broadcasted_iota(jnp.int32, (n_blocks, past_len), 0)
        ek = lax.broadcasted_iota(jnp.int32, (n_blocks, past_len), 1)
        expand = jnp.where(ek // MOBA_BLOCK == eb, 1.0, 0.0).astype(BF16)
        selk = _dot(jnp.where(sel, 1.0, 0.0).astype(BF16), expand)
        r = lax.broadcasted_iota(jnp.int32, (rows, 1), 0)
        t = r % n_q
        slope = _row_slopes(slopes_ref, r // n_q, N_HEADS_MOBA)
        kpos = lax.broadcasted_iota(jnp.int32, (rows, past_len), 1)
        s = jnp.where(selk > 0.5, s_ref[...] - slope * (past_len + t - kpos).astype(F32), NEG_INF)
        sn = _new_key_scores(qb, kn_ref, slope, t, n_q)
        m = jnp.maximum(jnp.max(s, axis=1, keepdims=True), jnp.max(sn, axis=1, keepdims=True))
        p = jnp.exp(s - m)
        pn = jnp.exp(sn - m)
        l_ref[...] = jnp.sum(p, axis=1, keepdims=True) + jnp.sum(pn, axis=1, keepdims=True)
        p_ref[...] = p.astype(BF16)
        pn_ref[...] = pn
        acc_ref[...] = jnp.zeros(acc_ref.shape, F32)

    @pl.when(c >= n_chunks)
    def _():
        cv = c - n_chunks
        acc = acc_ref[...]
        for p in range(PAGES_PER_STEP):
            pp = p_ref[:, pl.ds(pl.multiple_of(cv * CHUNK_KEYS + p * PAGE_SIZE, PAGE_SIZE), PAGE_SIZE)]
            vp = v_pages[p][0].reshape(GROUP_WIDTH, PAGE_SIZE)
            acc = acc + _dot_nt(pp, vp.astype(BF16))
        acc_ref[...] = acc

    @pl.when(c == 2 * n_chunks - 1)
    def _():
        vn = _pad_rows(vn_ref[0], SUBLANES)
        acc = acc_ref[...] + _dot(pn_ref[...].astype(BF16), vn.astype(BF16))
        acc = acc / l_ref[...]
        r = lax.broadcasted_iota(jnp.int32, (rows, GROUP_WIDTH), 0)
        cc = lax.broadcasted_iota(jnp.int32, (rows, GROUP_WIDTH), 1)
        acc = jnp.where(r // n_q == cc // HEAD_DIM, acc, 0.0)
        out = acc[0:n_q]
        for hh in range(1, N_HEADS_MOBA):
            out = out + acc[hh * n_q:(hh + 1) * n_q]
        o_ref[0] = out


def _page_specs(block_shape, n_grid, chunk_row):
    zeros = (0,) * (len(block_shape) - 1)

    def make(p):
        def index_map(*args):
            return (args[n_grid][chunk_row(*args), p],) + zeros
        return pl.BlockSpec(block_shape, index_map)
    return [make(p) for p in range(PAGES_PER_STEP)]


def _chunk_rows(n_seq, n_chunks, phase_offset, n_steps):
    rows = np.zeros((n_seq, n_steps), np.int32)
    for b in range(n_seq):
        for c in range(n_steps):
            rel = c - phase_offset
            if rel < 0:
                rows[b, c] = max(b - 1, 0) * n_chunks + n_chunks - 1
            else:
                rows[b, c] = b * n_chunks + min(rel, n_chunks - 1)
    return jnp.asarray(rows.reshape(-1))


KT_PAGE = (1, N_KT_HEADS, HEAD_DIM, PAGE_SIZE)


def _moba_attention_kernel(pt_ref, krow_ref, vrow_ref, slopes_ref, q_ref, mean_ref, k_ref, vt_ref,
                           sq_ref, kn_ref, vn_ref, *rest, n_blocks, n_q, n_chunks):
    k_pages = rest[:PAGES_PER_STEP]
    v_pages = rest[PAGES_PER_STEP:2 * PAGES_PER_STEP]
    o_ref, so_ref = rest[2 * PAGES_PER_STEP:2 * PAGES_PER_STEP + 2]
    kaug_ref, vtb_ref, s_ref, p_ref, gate_ref, acc_ref, l_ref, pn_ref = rest[2 * PAGES_PER_STEP + 2:]
    c = pl.program_id(1)
    _moba_sample_step(c, slopes_ref, sq_ref, kn_ref, vn_ref, k_pages, v_pages, so_ref,
                      s_ref, p_ref, gate_ref, acc_ref, l_ref, pn_ref, n_q, n_chunks)

    @pl.when(c % 2 == 0)
    def _():
        slot = pl.program_id(0) * n_chunks + c // 2
        _moba_prompt_step((slot // n_blocks) % (N_HEADS_MOBA // 2), slot % n_blocks, slopes_ref, q_ref, mean_ref,
                          k_ref, vt_ref, o_ref, kaug_ref, vtb_ref, n_blocks)


def _moba_attention(page_table, slopes, qa, means, ka_b, vat_b, sqa, ska_new, sva_new, cache_kt, cache_vt):
    b, s, _ = qa.shape
    n_blocks = s // MOBA_BLOCK
    sb, n_q, _ = sqa.shape
    n_pages = page_table.shape[1]
    n_chunks = n_pages // PAGES_PER_STEP
    past_len = n_pages * PAGE_SIZE
    rows = N_HEADS_MOBA * n_q
    n_steps = 2 * n_chunks
    half = N_HEADS_MOBA // 2
    assert b * half * n_blocks == sb * n_chunks, "one prompt step rides on every second sample step"
    k_rows = _chunk_rows(sb, n_chunks, 0, n_steps)
    v_rows = _chunk_rows(sb, n_chunks, n_chunks, n_steps)

    def prompt_step(bi, c):
        slot = bi * n_chunks + c // 2
        return slot // (half * n_blocks), (slot // n_blocks) % half, slot % n_blocks

    def by_prompt(fn):
        return lambda bi, c, *_: fn(*prompt_step(bi, c))

    blk_spec = pl.BlockSpec((1, MOBA_BLOCK, LANES), by_prompt(lambda pb, hp, i: (pb, i, hp)))
    tok_spec = pl.BlockSpec((1, n_q, GROUP_WIDTH), lambda bi, c, *_: (bi, 0, 0))
    return pl.pallas_call(
        functools.partial(_moba_attention_kernel, n_blocks=n_blocks, n_q=n_q, n_chunks=n_chunks),
        grid_spec=pltpu.PrefetchScalarGridSpec(
            num_scalar_prefetch=4,
            grid=(sb, n_steps),
            in_specs=[blk_spec,
                      pl.BlockSpec((1, n_blocks, LANES), by_prompt(lambda pb, hp, i: (pb, 0, hp))),
                      pl.BlockSpec((1, s, LANES), by_prompt(lambda pb, hp, i: (pb, 0, hp))),
                      pl.BlockSpec((1, 2, HEAD_DIM, s), by_prompt(lambda pb, hp, i: (pb, hp, 0, 0))),
                      tok_spec, tok_spec, tok_spec]
            + _page_specs(KT_PAGE, 2, lambda bi, c, pt, kr, vr, sl: kr[bi * n_steps + c])
            + _page_specs(KT_PAGE, 2, lambda bi, c, pt, kr, vr, sl: vr[bi * n_steps + c]),
            out_specs=[blk_spec, tok_spec],
            scratch_shapes=[pltpu.VMEM((2, s, LANES), BF16),
                            pltpu.VMEM((2 * n_blocks, HEAD_DIM, MOBA_BLOCK), BF16),
                            pltpu.VMEM((rows, past_len), F32), pltpu.VMEM((rows, past_len), BF16),
                            pltpu.VMEM((rows, past_len // MOBA_BLOCK), F32),
                            pltpu.VMEM((rows, GROUP_WIDTH), F32), pltpu.VMEM((rows, 1), F32),
                            pltpu.VMEM((rows, SUBLANES), F32)]),
        out_shape=[jax.ShapeDtypeStruct((b, s, GROUP_WIDTH), F32),
                   jax.ShapeDtypeStruct((sb, n_q, GROUP_WIDTH), F32)],
        compiler_params=_cparams(("arbitrary", "arbitrary")),
        name="moba_attention",
    )(page_table.reshape(sb * n_chunks, PAGES_PER_STEP), k_rows, v_rows, slopes, qa, means, ka_b, vat_b,
      sqa, ska_new, sva_new, *([cache_kt] * PAGES_PER_STEP), *([cache_vt] * PAGES_PER_STEP))


def _diff_sample_step(c, slopes_ref, lp_ref, q_ref, kn_ref, vn_ref, k_pages, v_pages, o_ref,
                      s_ref, m_ref, l_ref, acc_ref, n_q, n_chunks):
    rows = N_KT_HEADS * n_q
    head_rows = 2 * n_q
    e = 2 * HEAD_DIM
    past_len = n_chunks * CHUNK_KEYS
    qb = (_block_diag_q(q_ref[0]) * QK_SCALE).astype(BF16)

    @pl.when(c == 0)
    def _():
        m_ref[...] = jnp.full(m_ref.shape, NEG_INF, F32)
        l_ref[...] = jnp.zeros(l_ref.shape, F32)
        acc_ref[...] = jnp.zeros(acc_ref.shape, F32)

    r = lax.broadcasted_iota(jnp.int32, (rows, 1), 0)
    t = r % n_q
    slope = _row_slopes(slopes_ref, r // head_rows, N_HEADS_DIFF)

    for p in range(PAGES_PER_STEP):
        kp = k_pages[p][0].reshape(GROUP_WIDTH, PAGE_SIZE)
        s_ref[:, p * PAGE_SIZE:(p + 1) * PAGE_SIZE] = _dot(qb, kp.astype(BF16))
    kpos = c * CHUNK_KEYS + lax.broadcasted_iota(jnp.int32, (rows, CHUNK_KEYS), 1)
    s = s_ref[...] - slope * (past_len + t - kpos).astype(F32)
    m_old = m_ref[...]
    m_new = jnp.maximum(m_old, jnp.max(s, axis=1, keepdims=True))
    alpha = jnp.exp(m_old - m_new)
    pexp = jnp.exp(s - m_new)
    l_ref[...] = alpha * l_ref[...] + jnp.sum(pexp, axis=1, keepdims=True)
    m_ref[...] = m_new
    pb = pexp.astype(BF16)
    acc = alpha * acc_ref[...]
    parts = []
    for hh in range(N_HEADS_DIFF):
        a = acc[hh * head_rows:(hh + 1) * head_rows]
        for p in range(PAGES_PER_STEP):
            vh = v_pages[p][0, pl.ds(hh, PAGE_SIZE, stride=N_HEADS_DIFF), :].astype(BF16)
            a = a + _dot(pb[hh * head_rows:(hh + 1) * head_rows, p * PAGE_SIZE:(p + 1) * PAGE_SIZE], vh)
        parts.append(a)
    acc_ref[...] = jnp.concatenate(parts, axis=0)

    @pl.when(c == n_chunks - 1)
    def _():
        sn = _new_key_scores(qb, kn_ref, slope, t, n_q)
        m_o = m_ref[...]
        m_f = jnp.maximum(m_o, jnp.max(sn, axis=1, keepdims=True))
        al = jnp.exp(m_o - m_f)
        pn = jnp.exp(sn - m_f)
        l_f = al * l_ref[...] + jnp.sum(pn, axis=1, keepdims=True)
        pnb = pn.astype(BF16)
        vn = _pad_rows(vn_ref[0], SUBLANES).astype(BF16)
        a_old = al * acc_ref[...]
        lam = _lambda(lp_ref)
        outs = []
        for hh in range(N_HEADS_DIFF):
            rs = slice(hh * head_rows, (hh + 1) * head_rows)
            a = (a_old[rs] + _dot(pnb[rs], vn[:, hh * e:(hh + 1) * e])) / l_f[rs]
            outs.append(a[0:n_q] - lam * a[n_q:2 * n_q])
        o_ref[0] = jnp.concatenate(outs, axis=1)


def _diff_attention_kernel(pt_ref, slopes_ref, lp_ref, q_ref, k_ref, vt_ref, sq_ref, kn_ref, vn_ref, *rest,
                           n_blocks, n_q, n_chunks, n_sample_steps):
    k_pages = rest[:PAGES_PER_STEP]
    v_pages = rest[PAGES_PER_STEP:2 * PAGES_PER_STEP]
    o_ref, so_ref = rest[2 * PAGES_PER_STEP:2 * PAGES_PER_STEP + 2]
    kaug_ref, vtb_ref, s_ref, m_ref, l_ref, acc_ref = rest[2 * PAGES_PER_STEP + 2:]
    h = pl.program_id(1)
    i = pl.program_id(2)
    lin = (pl.program_id(0) * N_HEADS_DIFF + h) * n_blocks + i
    _diff_prompt_step(h, i, slopes_ref, lp_ref, q_ref, k_ref, vt_ref, o_ref, kaug_ref, vtb_ref, n_blocks)

    @pl.when(lin < n_sample_steps)
    def _():
        _diff_sample_step(lin % n_chunks, slopes_ref, lp_ref, sq_ref, kn_ref, vn_ref, k_pages, v_pages, so_ref,
                          s_ref, m_ref, l_ref, acc_ref, n_q, n_chunks)


def _diff_attention(page_table, slopes, lam_params, qd, kd_b, vdt_b, sqd, skd_new, svd_new, cache_kt, cache_v):
    b, s, _ = qd.shape
    n_blocks = s // MOBA_BLOCK
    sb, n_q, _ = sqd.shape
    n_chunks = page_table.shape[1] // PAGES_PER_STEP
    n_sample_steps = sb * n_chunks
    assert b * N_HEADS_DIFF * n_blocks >= n_sample_steps, "sample steps ride on the prompt steps"
    rows = N_KT_HEADS * n_q

    def chunk_row(bi, h, i, *_):
        return jnp.minimum((bi * N_HEADS_DIFF + h) * n_blocks + i, n_sample_steps - 1)

    blk_spec = pl.BlockSpec((1, MOBA_BLOCK, LANES), lambda bi, h, i, pt, sl: (bi, i, h))
    tok_spec = pl.BlockSpec((1, n_q, GROUP_WIDTH), lambda bi, h, i, pt, sl: (chunk_row(bi, h, i) // n_chunks, 0, 0))
    v_page = (1, PAGE_SIZE * N_HEADS_DIFF, 2 * HEAD_DIM)
    return pl.pallas_call(
        functools.partial(_diff_attention_kernel, n_blocks=n_blocks, n_q=n_q, n_chunks=n_chunks,
                          n_sample_steps=n_sample_steps),
        grid_spec=pltpu.PrefetchScalarGridSpec(
            num_scalar_prefetch=2,
            grid=(b, N_HEADS_DIFF, n_blocks),
            in_specs=[pl.BlockSpec((4, HEAD_DIM), lambda bi, h, i, pt, sl: (0, 0)), blk_spec,
                      pl.BlockSpec((1, s, LANES), lambda bi, h, i, pt, sl: (bi, 0, h)),
                      pl.BlockSpec((1, 1, 2 * HEAD_DIM, s), lambda bi, h, i, pt, sl: (bi, h, 0, 0)),
                      tok_spec, tok_spec, tok_spec]
            + _page_specs(KT_PAGE, 3, chunk_row) + _page_specs(v_page, 3, chunk_row),
            out_specs=[blk_spec, tok_spec],
            scratch_shapes=[pltpu.VMEM((2, s, LANES), BF16),
                            pltpu.VMEM((n_blocks, 2 * HEAD_DIM, MOBA_BLOCK), BF16),
                            pltpu.VMEM((rows, CHUNK_KEYS), F32), pltpu.VMEM((rows, 1), F32),
                            pltpu.VMEM((rows, 1), F32), pltpu.VMEM((rows, 2 * HEAD_DIM), F32)]),
        out_shape=[jax.ShapeDtypeStruct((b, s, GROUP_WIDTH), F32),
                   jax.ShapeDtypeStruct((sb, n_q, GROUP_WIDTH), F32)],
        compiler_params=_cparams(("arbitrary", "arbitrary", "arbitrary")),
        name="diff_attention",
    )(page_table.reshape(n_sample_steps, PAGES_PER_STEP), slopes, lam_params, qd, kd_b, vdt_b, sqd, skd_new, svd_new,
      *([cache_kt] * PAGES_PER_STEP), *([cache_v] * PAGES_PER_STEP))


def _post_kernel(oa_ref, od_ref, x_ref, gm_ref, gd_ref, wo_ref, o_ref):
    oa = _rms(oa_ref[...], gm_ref[...]).astype(BF16)
    y = x_ref[...] + _dot(oa, wo_ref[0:GROUP_WIDTH, :])
    od = od_ref[...]
    for h in range(N_HEADS_DIFF):
        w = 2 * HEAD_DIM
        odh = (_rms(od[:, h * w:(h + 1) * w], gd_ref[...]) * (1.0 - LAM_INIT)).astype(BF16)
        y = y + _dot(odh, wo_ref[GROUP_WIDTH + h * w:GROUP_WIDTH + (h + 1) * w, :])
    o_ref[...] = y


def _post(oa, od, x2d, g_moba, g_diff, wo_bf16, tm):
    t = x2d.shape[0]
    return pl.pallas_call(
        _post_kernel,
        grid=(t // tm,),
        in_specs=[pl.BlockSpec((tm, GROUP_WIDTH), lambda i: (i, 0)),
                  pl.BlockSpec((tm, GROUP_WIDTH), lambda i: (i, 0)),
                  pl.BlockSpec((tm, D_MODEL), lambda i: (i, 0)),
                  pl.BlockSpec((1, GROUP_WIDTH), lambda i: (0, 0)),
                  pl.BlockSpec((1, 2 * HEAD_DIM), lambda i: (0, 0)),
                  pl.BlockSpec((D_MODEL, D_MODEL), lambda i: (0, 0))],
        out_specs=pl.BlockSpec((tm, D_MODEL), lambda i: (i, 0)),
        out_shape=jax.ShapeDtypeStruct((t, D_MODEL), F32),
        compiler_params=_cparams(("arbitrary",)),
        name="post",
    )(oa, od, x2d, g_moba, g_diff, wo_bf16)


def _ffn_kernel(x_ref, gf_ref, wg_ref, wu_ref, cw_ref, cb_ref, wd_ref, e1_ref, e2_ref,
                o_ref, g_out_ref, gbuf_ref, carry_ref, *, tm, seq_len):
    long_seq = seq_len >= tm
    x = x_ref[0]
    h = _rms(x, gf_ref[...]).astype(BF16)
    if long_seq:
        @pl.when(pl.program_id(1) == 0)
        def _():
            carry_ref[...] = jnp.zeros(carry_ref.shape, F32)
    else:
        tmod = lax.broadcasted_iota(jnp.int32, (tm, 1), 0) % seq_len
    g_rows = g_out_ref.shape[1]
    y = x
    for fc in range(N_FF_CHUNKS):
        cs = slice(fc * FF_CHUNK, (fc + 1) * FF_CHUNK)
        g = _dot(h, wg_ref[:, cs])
        u = _dot(h, wu_ref[:, cs])
        gbuf_ref[SUBLANES:SUBLANES + tm, :] = g
        if long_seq:
            gbuf_ref[0:SUBLANES, :] = carry_ref[:, cs]
            carry_ref[:, cs] = g[tm - SUBLANES:tm]
        else:
            gbuf_ref[0:SUBLANES, :] = jnp.zeros((SUBLANES, FF_CHUNK), F32)
        gm1 = gbuf_ref[SUBLANES - 1:SUBLANES - 1 + tm, :]
        gm2 = gbuf_ref[SUBLANES - 2:SUBLANES - 2 + tm, :]
        if not long_seq:
            gm1 = jnp.where(tmod >= 1, gm1, e1_ref[0, :, cs])
            gm2 = jnp.where(tmod >= 2, gm2, e2_ref[0, :, cs])
        cw = cw_ref[:, cs]
        gc = cw[0:1] * gm2 + cw[1:2] * gm1 + cw[2:3] * g + cb_ref[:, cs]
        act = (jax.nn.gelu(gc) * u).astype(BF16)
        y = y + _dot(act, wd_ref[cs, :])
        g_out_ref[0, :, cs] = g[tm - g_rows:tm]
    o_ref[0] = y


def _ffn(x3d, g_ffn, wg, wu, conv_w, conv_b, wd, e1, e2, tm, seq_len):
    nb, s, _ = x3d.shape
    n_tiles = s // tm
    const = lambda shape: pl.BlockSpec(shape, lambda bi, i: (0,) * len(shape), pipeline_mode=pl.Buffered(1))
    e_rows = e1.shape[1]
    g_rows = SUBLANES if seq_len >= tm else tm
    return pl.pallas_call(
        functools.partial(_ffn_kernel, tm=tm, seq_len=seq_len),
        grid=(nb, n_tiles),
        in_specs=[pl.BlockSpec((1, tm, D_MODEL), lambda bi, i: (bi, i, 0)),
                  const((1, D_MODEL)), const((D_MODEL, D_FF)), const((D_MODEL, D_FF)),
                  const((3, D_FF)), const((1, D_FF)), const((D_FF, D_MODEL)),
                  pl.BlockSpec((1, e_rows, D_FF), lambda bi, i: (bi, 0, 0)),
                  pl.BlockSpec((1, e_rows, D_FF), lambda bi, i: (bi, 0, 0))],
        out_specs=[pl.BlockSpec((1, tm, D_MODEL), lambda bi, i: (bi, i, 0)),
                   pl.BlockSpec((1, g_rows, D_FF), lambda bi, i: (bi * n_tiles + i, 0, 0))],
        out_shape=[jax.ShapeDtypeStruct((nb, s, D_MODEL), F32),
                   jax.ShapeDtypeStruct((nb * n_tiles, g_rows, D_FF), F32)],
        scratch_shapes=[pltpu.VMEM((tm + SUBLANES, FF_CHUNK), F32), pltpu.VMEM((SUBLANES, D_FF), F32)],
        compiler_params=_cparams(("arbitrary", "arbitrary")),
        name="ffn",
    )(x3d, g_ffn, wg, wu, conv_w, conv_b, wd, e1, e2)


def _ple_kernel(x_ref, p_ref, gp_ref, wpg_ref, wpp_ref, gfin_ref, o_ref):
    x = x_ref[...]
    gate = jax.nn.sigmoid(_dot(_rms(x, gp_ref[...]).astype(BF16), wpg_ref[...]))
    x = x + _dot(p_ref[...].astype(BF16), wpp_ref[...]) * gate
    o_ref[...] = _rms(x, gfin_ref[...])


def _ple(x2d, p2d, g_ple, wpg, wpp, g_final, tm):
    t = x2d.shape[0]
    const = lambda shape: pl.BlockSpec(shape, lambda i: (0,) * len(shape))
    return pl.pallas_call(
        _ple_kernel,
        grid=(t // tm,),
        in_specs=[pl.BlockSpec((tm, D_MODEL), lambda i: (i, 0)),
                  pl.BlockSpec((tm, PLE_DIM), lambda i: (i, 0)),
                  const((1, D_MODEL)), const((D_MODEL, D_MODEL)), const((PLE_DIM, D_MODEL)),
                  const((1, D_MODEL))],
        out_specs=pl.BlockSpec((tm, D_MODEL), lambda i: (i, 0)),
        out_shape=jax.ShapeDtypeStruct((t, D_MODEL), F32),
        compiler_params=_cparams(("arbitrary",)),
        name="ple",
    )(x2d, p2d, g_ple, wpg, wpp, g_final)


def kernel(x_prompt, x_sample, cache_k_moba, cache_v_moba, cache_k_diff, cache_v_diff, state_conv, page_table,
           p_prompt, p_sample, g_attn, w_in, lam_params, g_moba_out, g_diff_sub, w_o, g_ffn, w_gate, w_up,
           conv_w, conv_b, w_down, g_ple, w_ple_gate, w_ple_proj, g_final):
    assert w_in.shape[0] == 1, "single-layer trunk"
    bp, sp, _ = x_prompt.shape
    bs, ss, _ = x_sample.shape
    assert sp >= 2 and ss >= 2, "the returned conv state is cut from this step's own gate rows"
    n_pool = cache_k_moba.shape[1]

    row = lambda v: v.reshape(1, -1)
    w_in_b = w_in[0].astype(BF16)
    col = lambda k: w_in_b[:, k * GROUP_WIDTH:(k + 1) * GROUP_WIDTH]
    w_o_b = w_o[0].astype(BF16)
    w_gate_b = w_gate[0].astype(BF16)
    w_up_b = w_up[0].astype(BF16)
    w_down_b = w_down[0].astype(BF16)
    w_pg_b = w_ple_gate[0].astype(BF16)
    w_pp_b = w_ple_proj[0].astype(BF16)
    slopes_a = _alibi_slopes(N_HEADS_MOBA)
    slopes_d = _alibi_slopes(N_HEADS_DIFF)
    lam_p = lam_params[0].astype(F32)

    def tail(oa, od, x3d, p3d, e1, e2, tm, seq_len):
        nb, s, _ = x3d.shape
        t = nb * s
        x1 = _post(oa.reshape(t, GROUP_WIDTH), od.reshape(t, GROUP_WIDTH), x3d.reshape(t, D_MODEL),
                   row(g_moba_out[0]), row(g_diff_sub[0]), w_o_b, min(tm, t))
        x2, g_tail = _ffn(x1.reshape(nb, s, D_MODEL), row(g_ffn[0]), w_gate_b, w_up_b, conv_w[0], row(conv_b[0]),
                          w_down_b, e1, e2, tm, seq_len)
        y = _ple(x2.reshape(t, D_MODEL), p3d.reshape(t, PLE_DIM), row(g_ple[0]), w_pg_b, w_pp_b, row(g_final),
                 min(tm, t))
        return y.reshape(nb, s, D_MODEL), g_tail

    w_rm = jnp.concatenate([col(0), col(3), col(5), col(1), col(4)], axis=1)
    w_t = jnp.concatenate([col(1), col(2), col(4), col(5)], axis=1).T
    qa, qd, vd, ka_b, kd_b, means, kat, vat, kdt, vat_b, vdt_b = _proj_prompt(x_prompt, row(g_attn[0]), w_rm, w_t, 512)
    ts = bs * ss
    sqa, ska, sva, sqd, skd, svd = _proj_rows(x_sample.reshape(1, ts, D_MODEL), row(g_attn[0]), w_in_b, ts)
    sh = lambda a: a.reshape(bs, ss, GROUP_WIDTH)
    kt_pages = lambda cch: jnp.moveaxis(cch[0].reshape(n_pool, PAGE_SIZE, N_KT_HEADS, HEAD_DIM), 1, 3)
    v_pages = cache_v_diff[0].reshape(n_pool, PAGE_SIZE * N_HEADS_DIFF, 2 * HEAD_DIM)

    oa, soa = _moba_attention(page_table, slopes_a, qa, means.reshape(bp, sp // MOBA_BLOCK, GROUP_WIDTH), ka_b, vat_b,
                              sh(sqa), sh(ska), sh(sva), kt_pages(cache_k_moba), kt_pages(cache_v_moba))
    od, sod = _diff_attention(page_table, slopes_d, lam_p, qd, kd_b, vdt_b, sh(sqd), sh(skd), sh(svd),
                              kt_pages(cache_k_diff), v_pages)

    tm_p = 512
    dummy = jnp.zeros((bp, SUBLANES, D_FF), F32)
    y_prompt, g_tail_p = tail(oa, od, x_prompt, p_prompt[0], dummy, dummy, tm_p, sp)
    conv_prompt = g_tail_p.reshape(bp, sp // tm_p, SUBLANES, D_FF)[:, -1, SUBLANES - 2:]
    outs_p = (jnp.transpose(kat, (0, 3, 1, 2))[None], jnp.transpose(vat, (0, 3, 1, 2))[None],
              jnp.transpose(kdt.reshape(bp, N_HEADS_DIFF, 2, HEAD_DIM, sp), (0, 4, 1, 2, 3))[None],
              vd.reshape(1, bp, sp, N_HEADS_DIFF, 2 * HEAD_DIM), conv_prompt[None])

    oa, od, ka, va, kd, vd = soa, sod, ska, sva, skd, svd
    prev = state_conv[0]
    zero = jnp.zeros((bs, ss - 1, D_FF), F32)
    e1 = jnp.concatenate([prev[:, 1:2], zero], axis=1).reshape(1, ts, D_FF)
    e2 = jnp.concatenate([prev, zero[:, 1:]], axis=1).reshape(1, ts, D_FF)
    y_sample, g_tail_s = tail(oa.reshape(1, ts, GROUP_WIDTH), od.reshape(1, ts, GROUP_WIDTH),
                              x_sample.reshape(1, ts, D_MODEL), p_sample[0].reshape(1, ts, PLE_DIM), e1, e2, ts, ss)
    conv_sample = g_tail_s.reshape(bs, ss, D_FF)[:, ss - 2:]
    outs_s = (ka.reshape(1, bs, ss, N_HEADS_MOBA, HEAD_DIM), va.reshape(1, bs, ss, N_HEADS_MOBA, HEAD_DIM),
              kd.reshape(1, bs, ss, N_HEADS_DIFF, 2, HEAD_DIM), vd.reshape(1, bs, ss, N_HEADS_DIFF, 2 * HEAD_DIM),
              conv_sample[None])
    return (y_prompt, y_sample.reshape(bs, ss, D_MODEL)) + outs_p + outs_s
```
